```python
import jax
import jax.numpy as jnp
from jax import lax
import numpy as np

D_MODEL = 4096
BATCH = 8
SEQ = 2048
DEPTH = 2

HEAD_DIM = 128
ROPE_THETA = 500000.0
ROPE_FRACTION = 4
NORM_EPS = 1e-6

DSA_HEADS = D_MODEL // (4 * HEAD_DIM)
IDX_HEADS = 16
IDX_DIM = 64
DSA_TOPK = 256
Q_BLOCK = 128

DIL_PATTERNS = ((128, 1), (512, 4), (2048, 16))
DIL_HEADS_PER_GROUP = D_MODEL // 1024
DIL_HEADS = len(DIL_PATTERNS) * DIL_HEADS_PER_GROUP
BAND_BLOCK = 128

POOL_WINDOWS = (2, 4, 8, 16)
POOL_WIDTH = D_MODEL - DSA_HEADS * HEAD_DIM - DIL_HEADS * HEAD_DIM
POOL_GROUP_WIDTH = POOL_WIDTH // len(POOL_WINDOWS)

N_BRANCHES = 3
A_OUT = DSA_HEADS * HEAD_DIM
B_OUT = DIL_HEADS_PER_GROUP * HEAD_DIM
C_OUT = POOL_WIDTH

SPLIT_SIZES = (DSA_HEADS * HEAD_DIM, HEAD_DIM, HEAD_DIM, IDX_HEADS * IDX_DIM, IDX_DIM, IDX_HEADS,
               DIL_HEADS * HEAD_DIM, DIL_HEADS * HEAD_DIM, DIL_HEADS * HEAD_DIM, POOL_WIDTH)
N_IN = sum(SPLIT_SIZES)

N_EXPERT_GROUPS = 4
EXPERTS_PER_GROUP = 8
N_EXPERTS = N_EXPERT_GROUPS * EXPERTS_PER_GROUP
TOP_K_IN_GROUP = 2
EXPERT_HIDDEN = 512
MOE_BLOCK = 128

kernel_name = 'hybrid_dsa_dilated_pool_hmoe'


def rms_norm(x, g):
    x32 = x.astype(jnp.float32)
    y = x32 * lax.rsqrt(jnp.mean(x32 * x32, axis=-1, keepdims=True) + NORM_EPS)
    return (y * g.astype(jnp.float32)).astype(x.dtype)


def partial_rope(x, positions):
    rd = x.shape[-1] // ROPE_FRACTION
    half = rd // 2
    inv_freq = ROPE_THETA ** (-jnp.arange(half, dtype=jnp.float32) * (2.0 / rd))
    ang = positions.astype(jnp.float32)[..., None] * inv_freq
    cos = jnp.cos(ang)[:, :, None, :]
    sin = jnp.sin(ang)[:, :, None, :]
    xr = x[..., :rd].astype(jnp.float32)
    x1, x2 = xr[..., :half], xr[..., half:]
    rot = jnp.concatenate([x1 * cos - x2 * sin, x2 * cos + x1 * sin], axis=-1)
    return jnp.concatenate([rot.astype(x.dtype), x[..., rd:]], axis=-1)


def dsa_attention(q, k, v, qi, ki, wi):
    bsz, seq = q.shape[0], q.shape[1]
    topk = min(DSA_TOPK, seq // 4)
    nqb = seq // Q_BLOCK
    spos = jnp.arange(seq)

    def block(args):
        qb, qib, wib, start = args
        tpos = start + jnp.arange(Q_BLOCK)
        dots = jnp.einsum('bqhd,bsd->bqhs', qib, ki,
                          preferred_element_type=jnp.float32) * (IDX_DIM ** -0.5)
        score = jnp.einsum('bqhs,bqh->bqs', jax.nn.relu(dots), wib.astype(jnp.float32))
        causal = spos[None, :] <= tpos[:, None]
        score = jnp.where(causal[None], score, -jnp.inf)
        _, sel = lax.top_k(score, topk)
        valid = sel <= tpos[None, :, None]
        k_sel = jax.vmap(lambda kk, ii: kk[ii])(k, sel)
        v_sel = jax.vmap(lambda vv, ii: vv[ii])(v, sel)
        s = jnp.einsum('bqhd,bqkd->bhqk', qb, k_sel,
                       preferred_element_type=jnp.float32) * (HEAD_DIM ** -0.5)
        s = jnp.where(valid[:, None], s, -jnp.inf)
        p = jax.nn.softmax(s, axis=-1).astype(v.dtype)
        return jnp.einsum('bhqk,bqkd->bqhd', p, v_sel)

    def to_blocks(t):
        return jnp.moveaxis(t.reshape(bsz, nqb, Q_BLOCK, *t.shape[2:]), 1, 0)

    out = lax.map(block, (to_blocks(q), to_blocks(qi), to_blocks(wi),
                          jnp.arange(nqb, dtype=jnp.int32) * Q_BLOCK))
    return jnp.moveaxis(out, 0, 1).reshape(bsz, seq, DSA_HEADS * HEAD_DIM)


def dilated_attention(q, k, v, window, dilation):
    bsz, seq, nh, dh = q.shape
    ls = seq // dilation
    nb = -(-ls // BAND_BLOCK)
    lp = nb * BAND_BLOCK
    steps = window // dilation

    def to_sub(t):
        t = t.reshape(bsz, ls, dilation, nh, dh).transpose(0, 2, 1, 3, 4)
        return jnp.pad(t, ((0, 0), (0, 0), (0, lp - ls), (0, 0), (0, 0)))

    def band(t):
        t = jnp.pad(t, ((0, 0), (0, 0), (BAND_BLOCK, 0), (0, 0), (0, 0)))
        t = t.reshape(bsz, dilation, nb + 1, BAND_BLOCK, nh, dh)
        return jnp.concatenate([t[:, :, :-1], t[:, :, 1:]], axis=3)

    qb = to_sub(q).reshape(bsz, dilation, nb, BAND_BLOCK, nh, dh)
    kb = band(to_sub(k))
    vb = band(to_sub(v))
    a = jnp.arange(BAND_BLOCK)[:, None]
    c = jnp.arange(2 * BAND_BLOCK)[None, :]
    dist = a - c + BAND_BLOCK
    kpos = (jnp.arange(nb)[:, None, None] - 1) * BAND_BLOCK + c[None]
    mask = (dist >= 0) & (dist <= steps) & (kpos >= 0)
    s = jnp.einsum('bxnqhd,bxnkhd->bxnhqk', qb, kb,
                   preferred_element_type=jnp.float32) * (dh ** -0.5)
    s = jnp.where(mask[None, None, :, None], s, -jnp.inf)
    lse = jax.nn.logsumexp(s, axis=-1)
    p = jnp.exp(s - lse[..., None]).astype(v.dtype)
    o = jnp.einsum('bxnhqk,bxnkhd->bxnqhd', p, vb)
    o = o.reshape(bsz, dilation, lp, nh, dh)[:, :, :ls]
    o = o.transpose(0, 2, 1, 3, 4).reshape(bsz, seq, nh, dh)
    lse = lse.transpose(0, 1, 2, 4, 3).reshape(bsz, dilation, lp, nh)[:, :, :ls]
    lse = lse.transpose(0, 2, 1, 3).reshape(bsz, seq, nh)
    return o, lse


def pool_mixer(c, w_pool, pool_scale):
    bsz, seq, _ = c.shape
    cg = c.reshape(bsz, seq, len(POOL_WINDOWS), POOL_GROUP_WIDTH)
    t1 = jnp.arange(1, seq + 1, dtype=jnp.float32)[None, :, None]
    outs = []
    for i, w in enumerate(POOL_WINDOWS):
        xi = cg[:, :, i].astype(jnp.float32)
        cs = lax.cumsum(xi, axis=1)
        prev = jnp.pad(cs[:, :-w], ((0, 0), (w, 0), (0, 0)))
        mean = (cs - prev) / jnp.minimum(t1, w)
        outs.append((mean - xi).astype(c.dtype) @ w_pool[i])
    return jnp.concatenate(outs, axis=-1) * pool_scale


def hybrid_mixer(h, positions, w_in, w_gate, b_gate, w_br_a, w_br_b, w_br_c, w_pool, pool_scale, w_out):
    bsz, seq, d = h.shape
    z = h @ w_in
    cuts, acc = [], 0
    for size in SPLIT_SIZES[:-1]:
        acc += size
        cuts.append(acc)
    qa, ka, va, qi, ki, wi, qb, kb, vb, c = jnp.split(z, cuts, axis=-1)

    qa = partial_rope(qa.reshape(bsz, seq, DSA_HEADS, HEAD_DIM), positions)
    ka = partial_rope(ka.reshape(bsz, seq, 1, HEAD_DIM), positions)[:, :, 0]
    qi = partial_rope(qi.reshape(bsz, seq, IDX_HEADS, IDX_DIM), positions)
    ki = partial_rope(ki.reshape(bsz, seq, 1, IDX_DIM), positions)[:, :, 0]
    wi = wi * (IDX_HEADS ** -0.5)
    a_out = dsa_attention(qa, ka, va, qi, ki, wi)

    qb = partial_rope(qb.reshape(bsz, seq, DIL_HEADS, HEAD_DIM), positions)
    kb = partial_rope(kb.reshape(bsz, seq, DIL_HEADS, HEAD_DIM), positions)
    vb = vb.reshape(bsz, seq, DIL_HEADS, HEAD_DIM)
    outs, lses = [], []
    for g, (window, dilation) in enumerate(DIL_PATTERNS):
        hs = slice(g * DIL_HEADS_PER_GROUP, (g + 1) * DIL_HEADS_PER_GROUP)
        o, lse = dilated_attention(qb[:, :, hs], kb[:, :, hs], vb[:, :, hs], window, dilation)
        outs.append(o)
        lses.append(lse)
    alpha = jax.nn.softmax(jnp.stack(lses, axis=0), axis=0)
    b_out = jnp.einsum('gbsh,gbshd->bshd', alpha.astype(h.dtype),
                       jnp.stack(outs, axis=0)).reshape(bsz, seq, B_OUT)

    c_out = pool_mixer(c, w_pool, pool_scale)

    gates = jax.nn.sigmoid((h @ w_gate + b_gate).astype(jnp.float32)).astype(h.dtype)
    gates = gates.reshape(bsz, seq, N_BRANCHES, d)
    merged = (gates[:, :, 0] * (a_out @ w_br_a)
              + gates[:, :, 1] * (b_out @ w_br_b)
              + gates[:, :, 2] * (c_out @ w_br_c))
    return merged @ w_out


def moe_ffn(h, w_rg, b_rg, w_re, b_re, w_e_gate, w_e_up, w_e_down):
    bsz, seq, d = h.shape
    n_tok = bsz * seq
    xt = h.reshape(n_tok, d)
    tok_ids = jnp.arange(n_tok, dtype=jnp.int32)
    lg = (xt @ w_rg).astype(jnp.float32) + b_rg.astype(jnp.float32)
    pg = jax.nn.softmax(lg, axis=-1)
    gsel = jnp.argmax(lg, axis=-1).astype(jnp.int32)
    pg_sel = pg[tok_ids, gsel]
    le = (xt @ w_re).astype(jnp.float32) + b_re.astype(jnp.float32)
    le = le.reshape(n_tok, N_EXPERT_GROUPS, EXPERTS_PER_GROUP)[tok_ids, gsel]
    pe = jax.nn.softmax(le, axis=-1)
    topv, topi = lax.top_k(pe, TOP_K_IN_GROUP)
    wts = pg_sel[:, None] * topv / jnp.sum(topv, axis=-1, keepdims=True)
    eid = gsel[:, None] * EXPERTS_PER_GROUP + topi.astype(jnp.int32)

    n_slots = n_tok * TOP_K_IN_GROUP
    e_flat = eid.reshape(-1)
    w_flat = wts.reshape(-1)
    tok_flat = jnp.repeat(tok_ids, TOP_K_IN_GROUP)
    order = jnp.argsort(e_flat)
    e_s, w_s, tok_s = e_flat[order], w_flat[order], tok_flat[order]
    counts = jnp.bincount(e_flat, length=N_EXPERTS)
    padded = ((counts + MOE_BLOCK - 1) // MOE_BLOCK) * MOE_BLOCK
    start_u = jnp.cumsum(counts) - counts
    end_p = jnp.cumsum(padded)
    start_p = end_p - padded
    dest = start_p[e_s] + (jnp.arange(n_slots) - start_u[e_s])
    n_blocks = -(-n_slots // MOE_BLOCK) + N_EXPERTS
    n_rows = n_blocks * MOE_BLOCK
    row_tok = jnp.zeros((n_rows,), jnp.int32).at[dest].set(tok_s)
    row_w = jnp.zeros((n_rows,), jnp.float32).at[dest].set(w_s)
    block_start = jnp.arange(n_blocks, dtype=end_p.dtype) * MOE_BLOCK
    block_e = jnp.minimum(jnp.searchsorted(end_p, block_start, side='right'), N_EXPERTS - 1)

    def expert_block(args):
        tok, wr, e = args
        xb = xt[tok]
        hid = jax.nn.silu(xb @ w_e_gate[e]) * (xb @ w_e_up[e])
        return (hid @ w_e_down[e]) * wr[:, None].astype(xt.dtype)

    y_rows = lax.map(expert_block, (row_tok.reshape(n_blocks, MOE_BLOCK),
                                    row_w.reshape(n_blocks, MOE_BLOCK), block_e))
    y = jax.ops.segment_sum(y_rows.reshape(n_rows, d), row_tok, num_segments=n_tok)
    return y.reshape(bsz, seq, d)


def setup_inputs(seed: int = 0) -> dict:
    key = jax.random.key(seed)
    ks = jax.random.split(key, 24)
    f32 = jnp.float32
    L, D = DEPTH, D_MODEL

    def dense(k, shape, fan_in):
        return jax.random.normal(k, shape, f32) * (fan_in ** -0.5)

    def gain(k, shape):
        return 1.0 + 0.05 * jax.random.normal(k, shape, f32)

    x = jax.random.normal(ks[0], (BATCH, SEQ, D), f32)
    offset = jax.random.randint(ks[1], (BATCH, 1), 0, 4096, dtype=jnp.int32)
    positions = offset + jnp.arange(SEQ, dtype=jnp.int32)[None, :]
    return {
        'x': x,
        'positions': positions,
        'norm1_g': gain(ks[2], (L, D)),
        'w_in': dense(ks[3], (L, D, N_IN), D),
        'w_gate': dense(ks[4], (L, D, N_BRANCHES * D), D),
        'b_gate': 0.02 * jax.random.normal(ks[5], (L, N_BRANCHES * D), f32),
        'w_br_a': dense(ks[6], (L, A_OUT, D), A_OUT),
        'w_br_b': dense(ks[7], (L, B_OUT, D), B_OUT),
        'w_br_c': dense(ks[8], (L, C_OUT, D), C_OUT),
        'w_pool': dense(ks[9], (L, len(POOL_WINDOWS), POOL_GROUP_WIDTH, POOL_GROUP_WIDTH), POOL_GROUP_WIDTH),
        'pool_scale': gain(ks[10], (L, POOL_WIDTH)),
        'w_out': dense(ks[11], (L, D, D), D),
        'norm2_g': gain(ks[12], (L, D)),
        'w_rg': dense(ks[13], (L, D, N_EXPERT_GROUPS), D),
        'b_rg': 0.01 * jax.random.normal(ks[14], (L, N_EXPERT_GROUPS), f32),
        'w_re': dense(ks[15], (L, D, N_EXPERTS), D),
        'b_re': 0.01 * jax.random.normal(ks[16], (L, N_EXPERTS), f32),
        'w_e_gate': dense(ks[17], (L, N_EXPERTS, D, EXPERT_HIDDEN), D),
        'w_e_up': dense(ks[18], (L, N_EXPERTS, D, EXPERT_HIDDEN), D),
        'w_e_down': dense(ks[19], (L, N_EXPERTS, EXPERT_HIDDEN, D), EXPERT_HIDDEN),
        'final_g': gain(ks[20], (D,)),
    }


def reference(x, positions, norm1_g, w_in, w_gate, b_gate, w_br_a, w_br_b, w_br_c, w_pool,
              pool_scale, w_out, norm2_g, w_rg, b_rg, w_re, b_re, w_e_gate, w_e_up, w_e_down,
              final_g):
    for l in range(DEPTH):
        h = rms_norm(x, norm1_g[l])
        x = x + hybrid_mixer(h, positions, w_in[l], w_gate[l], b_gate[l], w_br_a[l], w_br_b[l],
                             w_br_c[l], w_pool[l], pool_scale[l], w_out[l])
        h2 = rms_norm(x, norm2_g[l])
        x = x + moe_ffn(h2, w_rg[l], b_rg[l], w_re[l], b_re[l], w_e_gate[l], w_e_up[l], w_e_down[l])
    return rms_norm(x, final_g)
```

```python
import functools

import jax
import jax.numpy as jnp
from jax import lax
from jax.experimental import pallas as pl
from jax.experimental.pallas import tpu as pltpu

F32 = jnp.float32
BF16 = jnp.bfloat16
I32 = jnp.int32
U32 = jnp.uint32

LANES = 128
MXU_WIDTH = 256
VMEM_LIMIT = 56 * 1024 * 1024

HEAD_DIM = 128
ROPE_THETA = 500000.0
ROPE_FRACTION = 4
NORM_EPS = 1e-6
IDX_HEADS = 16
IDX_DIM = 64
DSA_TOPK = 256
Q_BLOCK = 128
DIL_PATTERNS = ((128, 1), (512, 4), (2048, 16))
POOL_WINDOWS = (2, 4, 8, 16)
N_EXPERT_GROUPS = 4
EXPERTS_PER_GROUP = 8
N_EXPERTS = N_EXPERT_GROUPS * EXPERTS_PER_GROUP
MOE_BLOCK = 128
NEG_INF = float("-inf")


def _cparams(*sem):
    return pltpu.CompilerParams(dimension_semantics=sem, vmem_limit_bytes=VMEM_LIMIT)


def _rope_table_kernel(pos_ref, f128_ref, s128_ref, f64_ref, s64_ref, c128, sn128, c64, sn64):
    p = pos_ref[...].astype(F32)
    a = p * f128_ref[...]
    c128[...] = jnp.cos(a)
    sn128[...] = jnp.sin(a) * s128_ref[...]
    a = p * f64_ref[...]
    c64[...] = jnp.cos(a)
    sn64[...] = jnp.sin(a) * s64_ref[...]


def _rope_lane_patterns():
    lane = jnp.arange(LANES)

    def pattern(width):
        rd = width // ROPE_FRACTION
        half = rd // 2
        inv = ROPE_THETA ** (-jnp.arange(half, dtype=F32) * (2.0 / rd))
        l = lane % width
        freq = jnp.where(l < rd, inv[l % half], 0.0).astype(F32)
        sign = jnp.where(l < half, -1.0, jnp.where(l < rd, 1.0, 0.0)).astype(F32)
        return freq[None, :], sign[None, :]

    return pattern(HEAD_DIM) + pattern(IDX_DIM)


def _rope_tables(positions):
    t = positions.size
    tm = min(t, 1024)
    f128, s128, f64, s64 = _rope_lane_patterns()
    vec = pl.BlockSpec((1, LANES), lambda i: (0, 0))
    tab = pl.BlockSpec((tm, LANES), lambda i: (i, 0))
    return pl.pallas_call(
        _rope_table_kernel,
        grid=(t // tm,),
        in_specs=[pl.BlockSpec((tm, 1), lambda i: (i, 0)), vec, vec, vec, vec],
        out_specs=[tab] * 4,
        out_shape=[jax.ShapeDtypeStruct((t, LANES), F32)] * 4,
        compiler_params=_cparams("arbitrary"),
    )(positions.reshape(t, 1), f128, s128, f64, s64)


def _rms(x, g):
    ms = jnp.mean(x * x, axis=-1, keepdims=True)
    return x * lax.rsqrt(ms + NORM_EPS) * g


def _norm_kernel(x_ref, g_ref, o_ref):
    o_ref[...] = _rms(x_ref[...], g_ref[...]).astype(o_ref.dtype)


def _rms_norm(x, g, out_dtype):
    t, d = x.shape
    tm = min(t, 512)
    return pl.pallas_call(
        _norm_kernel,
        grid=(t // tm,),
        in_specs=[pl.BlockSpec((tm, d), lambda i: (i, 0)), pl.BlockSpec((1, d), lambda i: (0, 0))],
        out_specs=pl.BlockSpec((tm, d), lambda i: (i, 0)),
        out_shape=jax.ShapeDtypeStruct((t, d), out_dtype),
        compiler_params=_cparams("arbitrary"),
    )(x, g.reshape(1, d))


def _proj_kernel(x_ref, w_ref, *rest, rope_width):
    acc = jnp.dot(x_ref[...], w_ref[...], preferred_element_type=F32)
    if rope_width is None:
        (o_ref,) = rest
        o_ref[...] = acc.astype(o_ref.dtype)
        return
    c_ref, s_ref, o_ref = rest
    half = rope_width // ROPE_FRACTION // 2
    c = c_ref[...]
    s = s_ref[...]
    lane = lax.broadcasted_iota(I32, c.shape, 1)
    first_half = (lane % rope_width) < half
    for g in range(acc.shape[1] // LANES):
        z = acc[:, g * LANES:(g + 1) * LANES]
        partner = jnp.where(first_half, pltpu.roll(z, LANES - half, 1), pltpu.roll(z, half, 1))
        o_ref[:, g * LANES:(g + 1) * LANES] = (z * c + partner * s).astype(o_ref.dtype)


def _project(h, w, out_dtype, tn, rope=None):
    t, k = h.shape
    n = w.shape[1]
    tm = min(t, 1024)
    in_specs = [pl.BlockSpec((tm, k), lambda i, j: (i, 0)), pl.BlockSpec((k, tn), lambda i, j: (0, j))]
    args = [h, w]
    if rope is not None:
        tab = pl.BlockSpec((tm, LANES), lambda i, j: (i, 0))
        in_specs += [tab, tab]
        args += [rope[1], rope[2]]
    return pl.pallas_call(
        functools.partial(_proj_kernel, rope_width=None if rope is None else rope[0]),
        grid=(t // tm, n // tn),
        in_specs=in_specs,
        out_specs=pl.BlockSpec((tm, tn), lambda i, j: (i, j)),
        out_shape=jax.ShapeDtypeStruct((t, n), out_dtype),
        compiler_params=_cparams("arbitrary", "arbitrary"),
    )(*args)


def _dsa_kernel(qa_ref, ka_ref, va_ref, qi_ref, kil_ref, kih_ref, wi_ref, o_ref, key_ref, bias_ref,
                *, topk, n_heads):
    nq, seq = key_ref.shape
    t0 = pl.program_id(1) * nq
    nt = (((1,), (1,)), ((), ()))
    w = wi_ref[...] * ((IDX_DIM ** -0.5) * (IDX_HEADS ** -0.5))
    kil = kil_ref[...]
    kih = kih_ref[...]
    score = jnp.zeros((nq, seq), F32)
    for j in range(IDX_HEADS // 2):
        qpair = qi_ref[:, j * LANES:(j + 1) * LANES]
        d_even = lax.dot_general(qpair, kil, nt, preferred_element_type=F32)
        d_odd = lax.dot_general(qpair, kih, nt, preferred_element_type=F32)
        score = score + jnp.maximum(d_even, 0.0) * w[:, 2 * j:2 * j + 1]
        score = score + jnp.maximum(d_odd, 0.0) * w[:, 2 * j + 1:2 * j + 2]
    tpos = t0 + lax.broadcasted_iota(I32, (nq, seq), 0)
    spos = lax.broadcasted_iota(I32, (nq, seq), 1)
    causal = spos <= tpos
    score = jnp.where(causal, score, NEG_INF)
    bits = pltpu.bitcast(score, I32)
    key_ref[...] = bits ^ ((bits >> 31) & jnp.int32(0x7FFFFFFF))

    def count_ge(cand):
        return jnp.sum(jnp.where(key_ref[...] >= cand, 1.0, 0.0), axis=1, keepdims=True)

    int_min = jnp.int32(-2 ** 31)
    thr = jnp.where(count_ge(jnp.zeros((nq, 1), I32)) >= topk, jnp.int32(0), int_min)

    def refine(it, thr):
        cand = thr | (jnp.int32(1) << (30 - it))
        return jnp.where(count_ge(cand) >= topk, cand, thr)

    thr = lax.fori_loop(0, 31, refine, thr)
    bias_ref[...] = jnp.where((key_ref[...] >= thr) & causal, 0.0, NEG_INF)
    ka = ka_ref[...]
    va = va_ref[...]
    for h in range(n_heads):
        q = qa_ref[:, h * HEAD_DIM:(h + 1) * HEAD_DIM]
        s = lax.dot_general(q, ka, nt, preferred_element_type=F32) * (HEAD_DIM ** -0.5) + bias_ref[...]
        m = jnp.max(s, axis=1, keepdims=True)
        p = jnp.exp(s - m)
        l = jnp.sum(p, axis=1, keepdims=True)
        o = jnp.dot(p.astype(BF16), va, preferred_element_type=F32)
        o_ref[:, h * HEAD_DIM:(h + 1) * HEAD_DIM] = (o / l).astype(o_ref.dtype)


def _dsa(zr, zi, zv, zc, bsz, seq, lay):
    t = bsz * seq
    nqb = seq // Q_BLOCK
    n_heads = lay["dsa_heads"]
    a_w = n_heads * HEAD_DIM
    qw = IDX_HEADS * IDX_DIM
    row = lambda b, i: b * nqb + i
    in_specs = [
        pl.BlockSpec((Q_BLOCK, a_w), lambda b, i: (row(b, i), 0)),
        pl.BlockSpec((seq, HEAD_DIM), lambda b, i: (b, lay["ka"] // HEAD_DIM)),
        pl.BlockSpec((seq, HEAD_DIM), lambda b, i: (b, lay["va"] // HEAD_DIM)),
        pl.BlockSpec((Q_BLOCK, qw), lambda b, i: (row(b, i), 0)),
        pl.BlockSpec((seq, LANES), lambda b, i: (b, lay["ki_lo"] // LANES)),
        pl.BlockSpec((seq, LANES), lambda b, i: (b, lay["ki_hi"] // LANES)),
        pl.BlockSpec((Q_BLOCK, LANES), lambda b, i: (row(b, i), lay["wi"] // LANES)),
    ]
    return pl.pallas_call(
        functools.partial(_dsa_kernel, topk=min(DSA_TOPK, seq // 4), n_heads=n_heads),
        grid=(bsz, nqb),
        in_specs=in_specs,
        out_specs=pl.BlockSpec((Q_BLOCK, a_w), lambda b, i: (row(b, i), 0)),
        out_shape=jax.ShapeDtypeStruct((t, a_w), BF16),
        scratch_shapes=[pltpu.VMEM((Q_BLOCK, seq), I32), pltpu.VMEM((Q_BLOCK, seq), F32)],
        compiler_params=_cparams("arbitrary", "arbitrary"),
    )(zr, zr, zv, zi, zi, zi, zc)


def _dil_kernel(*refs, seq):
    ng = len(DIL_PATTERNS)
    q_refs, k_refs, v_refs, o_ref = refs[:ng], refs[ng:2 * ng], refs[2 * ng:3 * ng], refs[3 * ng]
    nq = o_ref.shape[0]
    t0 = pl.program_id(2) * nq
    nt = (((1,), (1,)), ((), ()))
    outs, lses = [], []
    for g, (window, dil) in enumerate(DIL_PATTERNS):
        span = window + nq
        if span >= seq:
            start, span = 0, seq
            k = k_refs[g][...]
            v = v_refs[g][...]
        else:
            start = pl.multiple_of(jnp.clip(t0 - window, 0, seq - span), nq)
            k = k_refs[g][pl.ds(start, span), :]
            v = v_refs[g][pl.ds(start, span), :]
        s = lax.dot_general(q_refs[g][...], k, nt, preferred_element_type=F32) * (HEAD_DIM ** -0.5)
        tpos = t0 + lax.broadcasted_iota(I32, (nq, span), 0)
        spos = start + lax.broadcasted_iota(I32, (nq, span), 1)
        diff = tpos - spos
        valid = (diff.astype(U32) <= jnp.uint32(window)) & ((diff & (dil - 1)) == 0)
        s = jnp.where(valid, s, NEG_INF)
        m = jnp.max(s, axis=1, keepdims=True)
        p = jnp.exp(s - m)
        l = jnp.sum(p, axis=1, keepdims=True)
        outs.append(jnp.dot(p.astype(BF16), v, preferred_element_type=F32) / l)
        lses.append(m + jnp.log(l))
    mx = functools.reduce(jnp.maximum, lses)
    es = [jnp.exp(x - mx) for x in lses]
    tot = functools.reduce(lambda a, b: a + b, es)
    acc = sum((e / tot) * o for e, o in zip(es, outs))
    o_ref[...] = acc.astype(o_ref.dtype)


def _dilated(zr, zv, bsz, seq, lay):
    t = bsz * seq
    nqb = seq // Q_BLOCK
    hpg = lay["dil_heads_per_group"]
    ng = len(DIL_PATTERNS)

    def head_spec(base, g, rows):
        col = lambda s: base // HEAD_DIM + g * hpg + s
        if rows == Q_BLOCK:
            return pl.BlockSpec((Q_BLOCK, HEAD_DIM), lambda b, s, i: (b * nqb + i, col(s)))
        return pl.BlockSpec((seq, HEAD_DIM), lambda b, s, i: (b, col(s)))

    in_specs = ([head_spec(lay["qb"], g, Q_BLOCK) for g in range(ng)]
                + [head_spec(lay["kb"], g, seq) for g in range(ng)]
                + [head_spec(lay["vb"], g, seq) for g in range(ng)])
    return pl.pallas_call(
        functools.partial(_dil_kernel, seq=seq),
        grid=(bsz, hpg, nqb),
        in_specs=in_specs,
        out_specs=pl.BlockSpec((Q_BLOCK, HEAD_DIM), lambda b, s, i: (b * nqb + i, s)),
        out_shape=jax.ShapeDtypeStruct((t, hpg * HEAD_DIM), BF16),
        compiler_params=_cparams("arbitrary", "arbitrary", "arbitrary"),
    )(*([zr] * (2 * ng) + [zv] * ng))


def _pool_kernel(c_ref, wp_ref, ps_ref, o_ref):
    x = c_ref[...]
    row = lax.broadcasted_iota(I32, x.shape, 0)

    def run(window):
        acc = x
        k = 1
        while k < window:
            acc = acc + jnp.where(row >= k, pltpu.roll(acc, k, 0), 0.0)
            k *= 2
        denom = jnp.minimum(row[:, :1] + 1, window).astype(F32)
        y = (acc / denom - x).astype(BF16)
        o = jnp.dot(y, wp_ref[0], preferred_element_type=F32) * ps_ref[...]
        o_ref[...] = o.astype(o_ref.dtype)

    for gi, window in enumerate(POOL_WINDOWS):
        pl.when(pl.program_id(1) == gi)(functools.partial(run, window))


def _pool(zc, w_pool, pool_scale, bsz, seq):
    t = bsz * seq
    ng, gw = w_pool.shape[0], w_pool.shape[1]
    return pl.pallas_call(
        _pool_kernel,
        grid=(bsz, ng),
        in_specs=[pl.BlockSpec((seq, gw), lambda b, g: (b, g)),
                  pl.BlockSpec((1, gw, gw), lambda b, g: (g, 0, 0)),
                  pl.BlockSpec((1, gw), lambda b, g: (0, g))],
        out_specs=pl.BlockSpec((seq, gw), lambda b, g: (b, g)),
        out_shape=jax.ShapeDtypeStruct((t, ng * gw), BF16),
        compiler_params=_cparams("arbitrary", "arbitrary"),
    )(zc, w_pool, pool_scale.reshape(1, ng * gw))


def _sigmoid(x):
    return 1.0 / (1.0 + jnp.exp(-x))


def _merge_kernel(h_ref, wg0, wg1, wg2, bg0, bg1, bg2, a_ref, b_ref, c_ref, wa, wb, wc, o_ref):
    h = h_ref[...]
    acc = None
    for wg, bg, br, wbr in ((wg0, bg0, a_ref, wa), (wg1, bg1, b_ref, wb), (wg2, bg2, c_ref, wc)):
        gate = _sigmoid(jnp.dot(h, wg[...], preferred_element_type=F32) + bg[...])
        term = gate * jnp.dot(br[...], wbr[...], preferred_element_type=F32)
        acc = term if acc is None else acc + term
    o_ref[...] = acc.astype(o_ref.dtype)


def _merge(h, w_gate, b_gate, a_out, b_out, c_out, w_a, w_b, w_c):
    t, d = h.shape
    tm = min(t, 512)
    tn = 256
    nd = d // tn
    rows = lambda width: pl.BlockSpec((tm, width), lambda i, j: (i, 0))
    cols = lambda k, off: pl.BlockSpec((k, tn), lambda i, j: (0, off * nd + j))
    in_specs = ([rows(d)] + [cols(d, n) for n in range(3)] + [cols(1, n) for n in range(3)]
                + [rows(a_out.shape[1]), rows(b_out.shape[1]), rows(c_out.shape[1])]
                + [cols(a_out.shape[1], 0), cols(b_out.shape[1], 0), cols(c_out.shape[1], 0)])
    bg = b_gate.reshape(1, 3 * d)
    return pl.pallas_call(
        _merge_kernel,
        grid=(t // tm, nd),
        in_specs=in_specs,
        out_specs=pl.BlockSpec((tm, tn), lambda i, j: (i, j)),
        out_shape=jax.ShapeDtypeStruct((t, d), BF16),
        compiler_params=_cparams("arbitrary", "arbitrary"),
    )(h, w_gate, w_gate, w_gate, bg, bg, bg, a_out, b_out, c_out, w_a, w_b, w_c)


def _outproj_kernel(m_ref, w_ref, x_ref, o_ref):
    o_ref[...] = x_ref[...] + jnp.dot(m_ref[...], w_ref[...], preferred_element_type=F32)


def _outproj(merged, w_out, x):
    t, d = x.shape
    tm = min(t, 512)
    tn = 512
    return pl.pallas_call(
        _outproj_kernel,
        grid=(t // tm, d // tn),
        in_specs=[pl.BlockSpec((tm, d), lambda i, j: (i, 0)),
                  pl.BlockSpec((d, tn), lambda i, j: (0, j)),
                  pl.BlockSpec((tm, tn), lambda i, j: (i, j))],
        out_specs=pl.BlockSpec((tm, tn), lambda i, j: (i, j)),
        out_shape=jax.ShapeDtypeStruct((t, d), F32),
        compiler_params=_cparams("arbitrary", "arbitrary"),
    )(merged, w_out, x)


GROUP_LANE0 = N_EXPERTS


def _lane_min_index(mask, lane):
    return jnp.min(jnp.where(mask, lane, LANES), axis=1, keepdims=True)


def _router_kernel(x_ref, g_ref, wr_ref, br_ref, hp_ref, info_ref, cnt_ref, carry_ref):
    @pl.when(pl.program_id(0) == 0)
    def _():
        carry_ref[...] = jnp.zeros_like(carry_ref)

    hb = _rms(x_ref[...], g_ref[...]).astype(BF16)
    tm, d = hb.shape
    lo = pltpu.bitcast(hb[:, :d // 2].astype(F32), U32)
    hi = pltpu.bitcast(hb[:, d // 2:].astype(F32), U32)
    hp_ref[...] = hi | (lo >> 16)

    logits = jnp.dot(hb, wr_ref[...], preferred_element_type=F32) + br_ref[...]
    lane = lax.broadcasted_iota(I32, logits.shape, 1)
    is_group = (lane >= GROUP_LANE0) & (lane < GROUP_LANE0 + N_EXPERT_GROUPS)
    lg = jnp.where(is_group, logits, NEG_INF)
    gmax = jnp.max(lg, axis=1, keepdims=True)
    gsel = _lane_min_index(lg == gmax, lane) - GROUP_LANE0
    pg_sel = 1.0 / jnp.sum(jnp.where(is_group, jnp.exp(lg - gmax), 0.0), axis=1, keepdims=True)
    in_group = (lane < N_EXPERTS) & ((lane // EXPERTS_PER_GROUP) == gsel)
    le = jnp.where(in_group, logits, NEG_INF)
    ee = jnp.where(in_group, jnp.exp(le - jnp.max(le, axis=1, keepdims=True)), 0.0)
    pe = jnp.where(in_group, ee / jnp.sum(ee, axis=1, keepdims=True), -1.0)
    v0 = jnp.max(pe, axis=1, keepdims=True)
    e0 = _lane_min_index(pe == v0, lane)
    pe1 = jnp.where(lane == e0, -1.0, pe)
    v1 = jnp.max(pe1, axis=1, keepdims=True)
    e1 = _lane_min_index(pe1 == v1, lane)
    w0 = pg_sel * v0 / (v0 + v1)
    w1 = pg_sel * v1 / (v0 + v1)
    onehot = jnp.where((lane == e0) | (lane == e1), 1.0, 0.0)
    r = lax.broadcasted_iota(I32, (tm, tm), 0)
    c = lax.broadcasted_iota(I32, (tm, tm), 1)
    before = jnp.where(c < r, 1.0, 0.0).astype(BF16)
    prior = jnp.dot(before, onehot.astype(BF16), preferred_element_type=F32) + carry_ref[...]
    r0 = jnp.sum(jnp.where(lane == e0, prior, 0.0), axis=1, keepdims=True)
    r1 = jnp.sum(jnp.where(lane == e1, prior, 0.0), axis=1, keepdims=True)
    carry_ref[...] = carry_ref[...] + jnp.sum(onehot, axis=0, keepdims=True)
    cnt_ref[...] = carry_ref[...]
    info = jnp.zeros_like(logits)
    for k, val in enumerate((e0.astype(F32), e1.astype(F32), r0, r1, w0, w1)):
        info = jnp.where(lane == k, val, info)
    info_ref[...] = info


def _router(x, g, w_router, b_router):
    t, d = x.shape
    tm = min(t, 256)
    return pl.pallas_call(
        _router_kernel,
        grid=(t // tm,),
        in_specs=[pl.BlockSpec((tm, d), lambda i: (i, 0)),
                  pl.BlockSpec((1, d), lambda i: (0, 0)),
                  pl.BlockSpec((d, LANES), lambda i: (0, 0)),
                  pl.BlockSpec((1, LANES), lambda i: (0, 0))],
        out_specs=[pl.BlockSpec((tm, d // 2), lambda i: (i, 0)),
                   pl.BlockSpec((tm, LANES), lambda i: (i, 0)),
                   pl.BlockSpec((1, LANES), lambda i: (0, 0))],
        out_shape=[jax.ShapeDtypeStruct((t, d // 2), U32),
                   jax.ShapeDtypeStruct((t, LANES), F32),
                   jax.ShapeDtypeStruct((1, LANES), F32)],
        scratch_shapes=[pltpu.VMEM((1, LANES), F32)],
        compiler_params=_cparams("arbitrary"),
    )(x, g.reshape(1, d), w_router, b_router)


def _plan_kernel(cnt_ref, info_ref, dest_ref, meta_ref):
    lane = lax.broadcasted_iota(I32, (8, LANES), 1)
    cnt = jnp.broadcast_to(cnt_ref[...], (8, LANES))
    nblk = jnp.floor((cnt + (MOE_BLOCK - 1)) * (1.0 / MOE_BLOCK))
    r = lax.broadcasted_iota(I32, (LANES, LANES), 0)
    c = lax.broadcasted_iota(I32, (LANES, LANES), 1)
    upto = jnp.where(r <= c, 1.0, 0.0).astype(BF16)
    end_blk = jnp.dot(nblk.astype(BF16), upto, preferred_element_type=F32)
    start_row = (end_blk - nblk) * MOE_BLOCK

    info = info_ref[...]
    tl = lax.broadcasted_iota(I32, info.shape, 1)
    e0, e1, r0, r1 = (info[:, k:k + 1] for k in range(4))
    start = start_row[:1, :]
    d0 = jnp.sum(jnp.where(tl == e0.astype(I32), start, 0.0), axis=1, keepdims=True) + r0
    d1 = jnp.sum(jnp.where(tl == e1.astype(I32), start, 0.0), axis=1, keepdims=True) + r1
    dest_ref[...] = jnp.where(tl == 0, d0, jnp.where(tl == 1, d1, 0.0)).astype(I32)

    @pl.when(pl.program_id(0) == 0)
    def _():
        nb = meta_ref.shape[0]
        blk = lax.broadcasted_iota(I32, (nb, LANES), 0).astype(F32)
        ml = lax.broadcasted_iota(I32, (nb, LANES), 1)
        done = jnp.where((ml < N_EXPERTS) & (end_blk[:1, :] <= blk), 1.0, 0.0)
        block_e = jnp.minimum(jnp.sum(done, axis=1, keepdims=True), N_EXPERTS - 1.0)
        n_used = jnp.sum(jnp.where(lane[:1, :] == N_EXPERTS - 1, end_blk[:1, :], 0.0), axis=1, keepdims=True)
        meta_ref[...] = jnp.where(ml == 0, block_e, jnp.where(ml == 1, n_used, 0.0)).astype(I32)


def _plan(counts, info, n_blocks):
    t = info.shape[0]
    tm = min(t, 1024)
    nb = -(-n_blocks // 8) * 8
    return pl.pallas_call(
        _plan_kernel,
        grid=(t // tm,),
        in_specs=[pl.BlockSpec((1, LANES), lambda i: (0, 0)), pl.BlockSpec((tm, LANES), lambda i: (i, 0))],
        out_specs=[pl.BlockSpec((tm, LANES), lambda i: (i, 0)), pl.BlockSpec((nb, LANES), lambda i: (0, 0))],
        out_shape=[jax.ShapeDtypeStruct((t, LANES), I32), jax.ShapeDtypeStruct((nb, LANES), I32)],
        compiler_params=_cparams("arbitrary"),
    )(counts, info)


def _row_copy(src, src_row, dst, dst_row, n, sem):
    return pltpu.make_async_copy(src.at[pl.ds(src_row, n)], dst.at[pl.ds(dst_row, n)], sem)


def _dispatch_kernel(dest_ref, hp_ref, xs_in_ref, xs_ref, sem, *, tokens_per_step):
    del xs_in_ref
    base = pl.program_id(0) * tokens_per_step

    def issue(j, carry):
        tok = base + j
        _row_copy(hp_ref, tok, xs_ref, dest_ref[2 * tok], 1, sem).start()
        _row_copy(hp_ref, tok, xs_ref, dest_ref[2 * tok + 1], 1, sem).start()
        return carry

    lax.fori_loop(0, tokens_per_step, issue, 0)
    _row_copy(hp_ref, 0, xs_ref, 0, 2 * tokens_per_step, sem).wait()


def _dispatch(dest, hp, n_rows):
    t, wd = hp.shape
    tc = min(t, 256)
    xs0 = jnp.zeros((n_rows, wd), U32)
    return pl.pallas_call(
        functools.partial(_dispatch_kernel, tokens_per_step=tc),
        grid_spec=pltpu.PrefetchScalarGridSpec(
            num_scalar_prefetch=1,
            grid=(t // tc,),
            in_specs=[pl.BlockSpec(memory_space=pl.ANY), pl.BlockSpec(memory_space=pl.ANY)],
            out_specs=pl.BlockSpec(memory_space=pl.ANY),
            scratch_shapes=[pltpu.SemaphoreType.DMA(())],
        ),
        out_shape=jax.ShapeDtypeStruct((n_rows, wd), U32),
        input_output_aliases={2: 0},
        compiler_params=_cparams("arbitrary"),
    )(dest, hp, xs0)


def _expert_kernel(be_ref, nu_ref, xs_ref, wg_ref, wu_ref, wd_ref, o_ref):
    del be_ref

    @pl.when(pl.program_id(0) < nu_ref[0])
    def _():
        u = xs_ref[...]
        half = u.shape[1]
        lo = pltpu.bitcast(u << 16, F32).astype(BF16)
        hi = pltpu.bitcast(u & jnp.uint32(0xFFFF0000), F32).astype(BF16)

        def proj(w_ref):
            return (jnp.dot(lo, w_ref[0, :half, :], preferred_element_type=F32)
                    + jnp.dot(hi, w_ref[0, half:, :], preferred_element_type=F32))

        gate = proj(wg_ref)
        hid = gate * _sigmoid(gate) * proj(wu_ref)
        o_ref[...] = jnp.dot(hid.astype(BF16), wd_ref[0], preferred_element_type=F32)

    @pl.when(pl.program_id(0) >= nu_ref[0])
    def _():
        o_ref[...] = jnp.zeros_like(o_ref)


def _experts(block_e, n_used, xs, w_e_gate, w_e_up, w_e_down):
    n_rows, half = xs.shape
    d = 2 * half
    hid = w_e_gate.shape[2]
    n_blocks = n_rows // MOE_BLOCK
    blk = lambda i, be, nu: jnp.minimum(i, nu[0] - 1)
    return pl.pallas_call(
        _expert_kernel,
        grid_spec=pltpu.PrefetchScalarGridSpec(
            num_scalar_prefetch=2,
            grid=(n_blocks,),
            in_specs=[pl.BlockSpec((MOE_BLOCK, half), lambda i, be, nu: (blk(i, be, nu), 0)),
                      pl.BlockSpec((1, d, hid), lambda i, be, nu: (be[blk(i, be, nu)], 0, 0)),
                      pl.BlockSpec((1, d, hid), lambda i, be, nu: (be[blk(i, be, nu)], 0, 0)),
                      pl.BlockSpec((1, hid, d), lambda i, be, nu: (be[blk(i, be, nu)], 0, 0))],
            out_specs=pl.BlockSpec((MOE_BLOCK, d), lambda i, be, nu: (i, 0)),
        ),
        out_shape=jax.ShapeDtypeStruct((n_rows, d), F32),
        compiler_params=_cparams("arbitrary"),
    )(block_e, n_used, xs, w_e_gate, w_e_up, w_e_down)


def _combine_kernel(dest_ref, x_ref, info_ref, g_ref, y_ref, *rest, tokens_per_step, final):
    if final:
        o_ref, buf, sem = rest
    else:
        x2_ref, o_ref, buf, sem = rest
    base = pl.program_id(0) * tokens_per_step

    def issue(j, carry):
        tok = base + j
        _row_copy(y_ref, dest_ref[2 * tok], buf.at[0], j, 1, sem).start()
        _row_copy(y_ref, dest_ref[2 * tok + 1], buf.at[1], j, 1, sem).start()
        return carry

    lax.fori_loop(0, tokens_per_step, issue, 0)
    _row_copy(y_ref, 0, buf.at[0], 0, tokens_per_step, sem).wait()
    _row_copy(y_ref, 0, buf.at[1], 0, tokens_per_step, sem).wait()
    info = info_ref[...]
    x2 = x_ref[...] + (buf[0] * info[:, 4:5] + buf[1] * info[:, 5:6])
    if not final:
        x2_ref[...] = x2
    o_ref[...] = _rms(x2, g_ref[...]).astype(o_ref.dtype)


def _combine(dest, x, info, g_next, y_rows, final):
    t, d = x.shape
    tc = min(t, 128)
    tile = pl.BlockSpec((tc, d), lambda i, dr: (i, 0))
    out_specs = [tile] if final else [tile, tile]
    out_shape = ([jax.ShapeDtypeStruct((t, d), F32)] if final
                 else [jax.ShapeDtypeStruct((t, d), F32), jax.ShapeDtypeStruct((t, d), BF16)])
    return pl.pallas_call(
        functools.partial(_combine_kernel, tokens_per_step=tc, final=final),
        grid_spec=pltpu.PrefetchScalarGridSpec(
            num_scalar_prefetch=1,
            grid=(t // tc,),
            in_specs=[tile,
                      pl.BlockSpec((tc, LANES), lambda i, dr: (i, 0)),
                      pl.BlockSpec((1, d), lambda i, dr: (0, 0)),
                      pl.BlockSpec(memory_space=pl.ANY)],
            out_specs=out_specs,
            scratch_shapes=[pltpu.VMEM((2, tc, d), F32), pltpu.SemaphoreType.DMA(())],
        ),
        out_shape=out_shape,
        compiler_params=_cparams("arbitrary"),
    )(dest, x, info, g_next.reshape(1, d), y_rows)


def _pad_cols(w, width):
    return jnp.pad(w, ((0, 0), (0, width - w.shape[1])))


def _split_w_in(w_in, d):
    dsa_heads = d // (4 * HEAD_DIM)
    hpg = d // 1024
    dil_heads = len(DIL_PATTERNS) * hpg
    a_w, b_w = dsa_heads * HEAD_DIM, dil_heads * HEAD_DIM
    pool_w = d - a_w - b_w
    sizes = (a_w, HEAD_DIM, HEAD_DIM, IDX_HEADS * IDX_DIM, IDX_DIM, IDX_HEADS, b_w, b_w, b_w, pool_w)
    offs = [0]
    for s in sizes:
        offs.append(offs[-1] + s)
    qa, ka, va, qi, ki, wi, qb, kb, vb, c = (w_in[:, offs[n]:offs[n + 1]] for n in range(len(sizes)))
    zero64 = jnp.zeros_like(ki)
    mxu_pad = lambda w: _pad_cols(w, -(-w.shape[1] // MXU_WIDTH) * MXU_WIDTH)
    w_rope = mxu_pad(jnp.concatenate([qa, ka, qb, kb], axis=1))
    w_idx = mxu_pad(jnp.concatenate([qi, ki, zero64, zero64, ki], axis=1))
    w_val = mxu_pad(jnp.concatenate([va, vb], axis=1))
    w_c = mxu_pad(jnp.concatenate([c, wi], axis=1))
    lay = dict(dsa_heads=dsa_heads, dil_heads_per_group=hpg,
               ka=a_w, qb=a_w + HEAD_DIM, kb=a_w + HEAD_DIM + b_w,
               ki_lo=IDX_HEADS * IDX_DIM, ki_hi=IDX_HEADS * IDX_DIM + LANES,
               va=0, vb=HEAD_DIM, wi=pool_w)
    return tuple(w.astype(BF16) for w in (w_rope, w_idx, w_val, w_c)), lay


def _col_tile(n, cap=512):
    best = LANES
    for m in range(1, cap // LANES + 1):
        if n % (m * LANES) == 0:
            best = m * LANES
    return best


def kernel(x, positions, norm1_g, w_in, w_gate, b_gate, w_br_a, w_br_b, w_br_c, w_pool, pool_scale,
           w_out, norm2_g, w_rg, b_rg, w_re, b_re, w_e_gate, w_e_up, w_e_down, final_g):
    bsz, seq, d = x.shape
    depth = w_in.shape[0]
    t = bsz * seq
    n_blocks = -(-(2 * t) // MOE_BLOCK) + N_EXPERTS
    n_rows = n_blocks * MOE_BLOCK

    c128, s128, c64, s64 = _rope_tables(positions)
    xt = x.reshape(t, d)
    h = _rms_norm(xt, norm1_g[0], BF16)
    out = None
    for l in range(depth):
        (w_rope, w_idx, w_val, w_c), lay = _split_w_in(w_in[l], d)
        zr = _project(h, w_rope, BF16, _col_tile(w_rope.shape[1]), rope=(HEAD_DIM, c128, s128))
        zi = _project(h, w_idx, BF16, _col_tile(w_idx.shape[1]), rope=(IDX_DIM, c64, s64))
        zv = _project(h, w_val, BF16, _col_tile(w_val.shape[1]))
        zc = _project(h, w_c, F32, _col_tile(w_c.shape[1]))

        a_out = _dsa(zr, zi, zv, zc, bsz, seq, lay)
        b_out = _dilated(zr, zv, bsz, seq, lay)
        c_out = _pool(zc, w_pool[l].astype(BF16), pool_scale[l], bsz, seq)

        merged = _merge(h, w_gate[l].astype(BF16), b_gate[l], a_out, b_out, c_out,
                        w_br_a[l].astype(BF16), w_br_b[l].astype(BF16), w_br_c[l].astype(BF16))
        x1 = _outproj(merged, w_out[l].astype(BF16), xt)

        w_router = _pad_cols(jnp.concatenate([w_re[l], w_rg[l]], axis=1), LANES).astype(BF16)
        b_router = _pad_cols(jnp.concatenate([b_re[l], b_rg[l]])[None, :], LANES)
        hp, info, counts = _router(x1, norm2_g[l], w_router, b_router)
        dest2, meta = _plan(counts, info, n_blocks)
        dest = dest2[:, :2].reshape(2 * t)
        block_e = meta[:n_blocks, 0]
        n_used = meta[:1, 1]
        xs = _dispatch(dest, hp, n_rows)
        y_rows = _experts(block_e, n_used, xs, w_e_gate[l].astype(BF16), w_e_up[l].astype(BF16),
                          w_e_down[l].astype(BF16))
        final = l == depth - 1
        g_next = final_g if final else norm1_g[l + 1]
        res = _combine(dest, x1, info, g_next, y_rows, final)
        if final:
            out = res[0]
        else:
            xt, h = res
    return out.reshape(bsz, seq, d)
```

```python
import functools

import jax
import jax.numpy as jnp
from jax import lax
from jax.experimental import pallas as pl
from jax.experimental.pallas import tpu as pltpu

F32 = jnp.float32
BF16 = jnp.bfloat16
I32 = jnp.int32
U32 = jnp.uint32

LANES = 128
MXU_WIDTH = 256
VMEM_LIMIT = 56 * 1024 * 1024

HEAD_DIM = 128
ROPE_THETA = 500000.0
ROPE_FRACTION = 4
NORM_EPS = 1e-6
IDX_HEADS = 16
IDX_DIM = 64
DSA_TOPK = 256
Q_BLOCK = 128
DIL_PATTERNS = ((128, 1), (512, 4), (2048, 16))
POOL_WINDOWS = (2, 4, 8, 16)
N_EXPERT_GROUPS = 4
EXPERTS_PER_GROUP = 8
N_EXPERTS = N_EXPERT_GROUPS * EXPERTS_PER_GROUP
MOE_BLOCK = 128
NEG_INF = float("-inf")


def _cparams(*sem):
    return pltpu.CompilerParams(dimension_semantics=sem, vmem_limit_bytes=VMEM_LIMIT)


def _rope_table_kernel(pos_ref, f128_ref, s128_ref, f64_ref, s64_ref, c128, sn128, c64, sn64):
    p = pos_ref[...].astype(F32)
    a = p * f128_ref[...]
    c128[...] = jnp.cos(a)
    sn128[...] = jnp.sin(a) * s128_ref[...]
    a = p * f64_ref[...]
    c64[...] = jnp.cos(a)
    sn64[...] = jnp.sin(a) * s64_ref[...]


def _rope_lane_patterns():
    lane = jnp.arange(LANES)

    def pattern(width):
        rd = width // ROPE_FRACTION
        half = rd // 2
        inv = ROPE_THETA ** (-jnp.arange(half, dtype=F32) * (2.0 / rd))
        l = lane % width
        freq = jnp.where(l < rd, inv[l % half], 0.0).astype(F32)
        sign = jnp.where(l < half, -1.0, jnp.where(l < rd, 1.0, 0.0)).astype(F32)
        return freq[None, :], sign[None, :]

    return pattern(HEAD_DIM) + pattern(IDX_DIM)


def _rope_tables(positions):
    t = positions.size
    tm = min(t, 1024)
    f128, s128, f64, s64 = _rope_lane_patterns()
    vec = pl.BlockSpec((1, LANES), lambda i: (0, 0))
    tab = pl.BlockSpec((tm, LANES), lambda i: (i, 0))
    return pl.pallas_call(
        _rope_table_kernel,
        grid=(t // tm,),
        in_specs=[pl.BlockSpec((tm, 1), lambda i: (i, 0)), vec, vec, vec, vec],
        out_specs=[tab] * 4,
        out_shape=[jax.ShapeDtypeStruct((t, LANES), F32)] * 4,
        compiler_params=_cparams("arbitrary"),
    )(positions.reshape(t, 1), f128, s128, f64, s64)


def _rms(x, g):
    ms = jnp.mean(x * x, axis=-1, keepdims=True)
    return x * lax.rsqrt(ms + NORM_EPS) * g


def _norm_kernel(x_ref, g_ref, o_ref):
    o_ref[...] = _rms(x_ref[...], g_ref[...]).astype(o_ref.dtype)


def _rms_norm(x, g, out_dtype):
    t, d = x.shape
    tm = min(t, 512)
    return pl.pallas_call(
        _norm_kernel,
        grid=(t // tm,),
        in_specs=[pl.BlockSpec((tm, d), lambda i: (i, 0)), pl.BlockSpec((1, d), lambda i: (0, 0))],
        out_specs=pl.BlockSpec((tm, d), lambda i: (i, 0)),
        out_shape=jax.ShapeDtypeStruct((t, d), out_dtype),
        compiler_params=_cparams("arbitrary"),
    )(x, g.reshape(1, d))


def _proj_kernel(x_ref, w_ref, *rest, rope_width):
    acc = jnp.dot(x_ref[...], w_ref[...], preferred_element_type=F32)
    if rope_width is None:
        (o_ref,) = rest
        o_ref[...] = acc.astype(o_ref.dtype)
        return
    c_ref, s_ref, o_ref = rest
    half = rope_width // ROPE_FRACTION // 2
    c = c_ref[...]
    s = s_ref[...]
    lane = lax.broadcasted_iota(I32, c.shape, 1)
    first_half = (lane % rope_width) < half
    for g in range(acc.shape[1] // LANES):
        z = acc[:, g * LANES:(g + 1) * LANES]
        partner = jnp.where(first_half, pltpu.roll(z, LANES - half, 1), pltpu.roll(z, half, 1))
        o_ref[:, g * LANES:(g + 1) * LANES] = (z * c + partner * s).astype(o_ref.dtype)


def _project(h, w, out_dtype, tn, rope=None):
    t, k = h.shape
    n = w.shape[1]
    tm = min(t, 1024)
    in_specs = [pl.BlockSpec((tm, k), lambda i, j: (i, 0)), pl.BlockSpec((k, tn), lambda i, j: (0, j))]
    args = [h, w]
    if rope is not None:
        tab = pl.BlockSpec((tm, LANES), lambda i, j: (i, 0))
        in_specs += [tab, tab]
        args += [rope[1], rope[2]]
    return pl.pallas_call(
        functools.partial(_proj_kernel, rope_width=None if rope is None else rope[0]),
        grid=(t // tm, n // tn),
        in_specs=in_specs,
        out_specs=pl.BlockSpec((tm, tn), lambda i, j: (i, j)),
        out_shape=jax.ShapeDtypeStruct((t, n), out_dtype),
        compiler_params=_cparams("arbitrary", "arbitrary"),
    )(*args)


KEY_CHUNK = 256
SUBLANES = 8


def _sublane_allreduce(x, op):
    for shift in (4, 2, 1):
        x = op(x, pltpu.roll(x, shift, 0))
    return x


def _tree_reduce(x, op):
    while x.shape[0] > 1:
        half = x.shape[0] // 2
        x = op(x[:half], x[half:])
    return x[0]


def _vregs(x):
    return x.reshape(x.shape[0] // SUBLANES, SUBLANES, LANES)


def _dsa_kernel(qa_ref, ka_ref, va_ref, qi_ref, kil_ref, kih_ref, wi_ref, o_ref,
                key_ref, vat_ref, acc_ref, m_ref, l_ref, *, topk, n_heads):
    nq = o_ref.shape[0]
    ck = KEY_CHUNK
    i = pl.program_id(1)
    t0 = i * nq
    n_chunks = (t0 + nq + ck - 1) // ck
    nt = (((1,), (1,)), ((), ()))

    @pl.when(i == 0)
    def _():
        for c in range(vat_ref.shape[0]):
            vat_ref[c] = va_ref[c * ck:(c + 1) * ck, :].astype(F32).T.astype(BF16)

    def causal_mask(c):
        shape = (ck // SUBLANES, SUBLANES, nq)
        spos = (c * ck + lax.broadcasted_iota(I32, shape, 0) * SUBLANES
                + lax.broadcasted_iota(I32, shape, 1))
        tpos = t0 + lax.broadcasted_iota(I32, shape, 2)
        return spos <= tpos

    wt = (wi_ref[...] * ((IDX_DIM ** -0.5) * (IDX_HEADS ** -0.5))).T
    w8 = [jnp.broadcast_to(wt[h:h + 1, :], (SUBLANES, LANES)) for h in range(IDX_HEADS)]
    q_stack = [jnp.concatenate([qi_ref[:, (2 * m) * LANES:(2 * m + 1) * LANES],
                                qi_ref[:, (2 * m + 1) * LANES:(2 * m + 2) * LANES]], axis=0)
               for m in range(IDX_HEADS // 4)]

    def score_chunk(c, carry):
        r0 = pl.multiple_of(c * ck, ck)
        kil = kil_ref[pl.ds(r0, ck), :]
        kih = kih_ref[pl.ds(r0, ck), :]
        acc = jnp.zeros((ck // SUBLANES, SUBLANES, LANES), F32)
        for m in range(IDX_HEADS // 4):
            d_even = lax.dot_general(kil, q_stack[m], nt, preferred_element_type=F32)
            d_odd = lax.dot_general(kih, q_stack[m], nt, preferred_element_type=F32)
            for dots, h in ((d_even[:, :nq], 4 * m), (d_odd[:, :nq], 4 * m + 1),
                            (d_even[:, nq:], 4 * m + 2), (d_odd[:, nq:], 4 * m + 3)):
                acc = acc + jnp.maximum(_vregs(dots), 0.0) * w8[h][None]
        score = jnp.where(causal_mask(c), acc, NEG_INF).reshape(ck, nq)
        bits = pltpu.bitcast(score, I32)
        key_ref[c] = bits ^ ((bits >> 31) & jnp.int32(0x7FFFFFFF))
        return carry

    lax.fori_loop(0, n_chunks, score_chunk, 0)

    def count_ge(cand):
        def body(c, cnt):
            return cnt + _tree_reduce(jnp.where(_vregs(key_ref[c]) >= cand[None], 1.0, 0.0), jnp.add)
        cnt = lax.fori_loop(0, n_chunks, body, jnp.zeros((SUBLANES, LANES), F32))
        return _sublane_allreduce(cnt, jnp.add)

    int_min = jnp.int32(-2 ** 31)
    thr = jnp.where(count_ge(jnp.zeros((SUBLANES, LANES), I32)) >= topk, jnp.int32(0), int_min)

    def refine(it, thr):
        cand = thr | (jnp.int32(1) << (30 - it))
        return jnp.where(count_ge(cand) >= topk, cand, thr)

    thr = lax.fori_loop(0, 31, refine, thr)

    m_ref[...] = jnp.full(m_ref.shape, NEG_INF, F32)
    l_ref[...] = jnp.zeros(l_ref.shape, F32)
    acc_ref[...] = jnp.zeros(acc_ref.shape, F32)

    def attend(c, carry):
        r0 = pl.multiple_of(c * ck, ck)
        keep = (_vregs(key_ref[c]) >= thr[None]) & causal_mask(c)
        bias = jnp.where(keep, 0.0, NEG_INF)
        ka = ka_ref[pl.ds(r0, ck), :]
        vat = vat_ref[c]
        for h in range(n_heads):
            q = qa_ref[:, h * HEAD_DIM:(h + 1) * HEAD_DIM]
            s = _vregs(lax.dot_general(ka, q, nt, preferred_element_type=F32) * (HEAD_DIM ** -0.5)) + bias
            m_old = m_ref[h]
            m_new = jnp.maximum(m_old, _sublane_allreduce(_tree_reduce(s, jnp.maximum), jnp.maximum))
            m_safe = jnp.where(m_new == NEG_INF, 0.0, m_new)
            alpha = jnp.exp(m_old - m_safe)
            p = jnp.exp(s - m_safe[None])
            l_ref[h] = alpha * l_ref[h] + _tree_reduce(p, jnp.add)
            pv = jnp.dot(vat, p.reshape(ck, nq).astype(BF16), preferred_element_type=F32)
            acc_ref[h] = (_vregs(acc_ref[h]) * alpha[None]).reshape(HEAD_DIM, nq) + pv
            m_ref[h] = m_new
        return carry

    lax.fori_loop(0, n_chunks, attend, 0)
    for h in range(n_heads):
        l = _sublane_allreduce(l_ref[h], jnp.add)
        o = (_vregs(acc_ref[h]) / l[None]).reshape(HEAD_DIM, nq)
        o_ref[:, h * HEAD_DIM:(h + 1) * HEAD_DIM] = o.T.astype(o_ref.dtype)


def _dsa(zr, zi, zv, zc, bsz, seq, lay):
    t = bsz * seq
    nqb = seq // Q_BLOCK
    n_heads = lay["dsa_heads"]
    a_w = n_heads * HEAD_DIM
    qw = IDX_HEADS * IDX_DIM
    nck = seq // KEY_CHUNK
    row = lambda b, i: b * nqb + i
    in_specs = [
        pl.BlockSpec((Q_BLOCK, a_w), lambda b, i: (row(b, i), 0)),
        pl.BlockSpec((seq, HEAD_DIM), lambda b, i: (b, lay["ka"] // HEAD_DIM)),
        pl.BlockSpec((seq, HEAD_DIM), lambda b, i: (b, lay["va"] // HEAD_DIM)),
        pl.BlockSpec((Q_BLOCK, qw), lambda b, i: (row(b, i), 0)),
        pl.BlockSpec((seq, LANES), lambda b, i: (b, lay["ki_lo"] // LANES)),
        pl.BlockSpec((seq, LANES), lambda b, i: (b, lay["ki_hi"] // LANES)),
        pl.BlockSpec((Q_BLOCK, LANES), lambda b, i: (row(b, i), lay["wi"] // LANES)),
    ]
    return pl.pallas_call(
        functools.partial(_dsa_kernel, topk=min(DSA_TOPK, seq // 4), n_heads=n_heads),
        grid=(bsz, nqb),
        in_specs=in_specs,
        out_specs=pl.BlockSpec((Q_BLOCK, a_w), lambda b, i: (row(b, i), 0)),
        out_shape=jax.ShapeDtypeStruct((t, a_w), BF16),
        scratch_shapes=[pltpu.VMEM((nck, KEY_CHUNK, Q_BLOCK), I32),
                        pltpu.VMEM((nck, HEAD_DIM, KEY_CHUNK), BF16),
                        pltpu.VMEM((n_heads, HEAD_DIM, Q_BLOCK), F32),
                        pltpu.VMEM((n_heads, SUBLANES, Q_BLOCK), F32),
                        pltpu.VMEM((n_heads, SUBLANES, Q_BLOCK), F32)],
        compiler_params=_cparams("arbitrary", "arbitrary"),
    )(zr, zr, zv, zi, zi, zi, zc)


def _dil_kernel(*refs, seq):
    ng = len(DIL_PATTERNS)
    q_refs, k_refs, v_refs, o_ref = refs[:ng], refs[ng:2 * ng], refs[2 * ng:3 * ng], refs[3 * ng]
    nq = o_ref.shape[0]
    t0 = pl.program_id(2) * nq
    nt = (((1,), (1,)), ((), ()))
    outs, lses = [], []
    for g, (window, dil) in enumerate(DIL_PATTERNS):
        span = window + nq
        if span >= seq:
            start, span = 0, seq
            k = k_refs[g][...]
            v = v_refs[g][...]
        else:
            start = pl.multiple_of(jnp.clip(t0 - window, 0, seq - span), nq)
            k = k_refs[g][pl.ds(start, span), :]
            v = v_refs[g][pl.ds(start, span), :]
        s = lax.dot_general(q_refs[g][...], k, nt, preferred_element_type=F32) * (HEAD_DIM ** -0.5)
        tpos = t0 + lax.broadcasted_iota(I32, (nq, span), 0)
        spos = start + lax.broadcasted_iota(I32, (nq, span), 1)
        diff = tpos - spos
        valid = (diff.astype(U32) <= jnp.uint32(window)) & ((diff & (dil - 1)) == 0)
        s = jnp.where(valid, s, NEG_INF)
        m = jnp.max(s, axis=1, keepdims=True)
        p = jnp.exp(s - m)
        l = jnp.sum(p, axis=1, keepdims=True)
        outs.append(jnp.dot(p.astype(BF16), v, preferred_element_type=F32) / l)
        lses.append(m + jnp.log(l))
    mx = functools.reduce(jnp.maximum, lses)
    es = [jnp.exp(x - mx) for x in lses]
    tot = functools.reduce(lambda a, b: a + b, es)
    acc = sum((e / tot) * o for e, o in zip(es, outs))
    o_ref[...] = acc.astype(o_ref.dtype)


def _dilated(zr, zv, bsz, seq, lay):
    t = bsz * seq
    nqb = seq // Q_BLOCK
    hpg = lay["dil_heads_per_group"]
    ng = len(DIL_PATTERNS)

    def head_spec(base, g, rows):
        col = lambda s: base // HEAD_DIM + g * hpg + s
        if rows == Q_BLOCK:
            return pl.BlockSpec((Q_BLOCK, HEAD_DIM), lambda b, s, i: (b * nqb + i, col(s)))
        return pl.BlockSpec((seq, HEAD_DIM), lambda b, s, i: (b, col(s)))

    in_specs = ([head_spec(lay["qb"], g, Q_BLOCK) for g in range(ng)]
                + [head_spec(lay["kb"], g, seq) for g in range(ng)]
                + [head_spec(lay["vb"], g, seq) for g in range(ng)])
    return pl.pallas_call(
        functools.partial(_dil_kernel, seq=seq),
        grid=(bsz, hpg, nqb),
        in_specs=in_specs,
        out_specs=pl.BlockSpec((Q_BLOCK, HEAD_DIM), lambda b, s, i: (b * nqb + i, s)),
        out_shape=jax.ShapeDtypeStruct((t, hpg * HEAD_DIM), BF16),
        compiler_params=_cparams("arbitrary", "arbitrary", "arbitrary"),
    )(*([zr] * (2 * ng) + [zv] * ng))


def _pool_kernel(c_ref, wp_ref, ps_ref, o_ref):
    x = c_ref[...]
    row = lax.broadcasted_iota(I32, x.shape, 0)

    def run(window):
        acc = x
        k = 1
        while k < window:
            acc = acc + jnp.where(row >= k, pltpu.roll(acc, k, 0), 0.0)
            k *= 2
        denom = jnp.minimum(row[:, :1] + 1, window).astype(F32)
        y = (acc / denom - x).astype(BF16)
        o = jnp.dot(y, wp_ref[0], preferred_element_type=F32) * ps_ref[...]
        o_ref[...] = o.astype(o_ref.dtype)

    for gi, window in enumerate(POOL_WINDOWS):
        pl.when(pl.program_id(1) == gi)(functools.partial(run, window))


def _pool(zc, w_pool, pool_scale, bsz, seq):
    t = bsz * seq
    ng, gw = w_pool.shape[0], w_pool.shape[1]
    return pl.pallas_call(
        _pool_kernel,
        grid=(bsz, ng),
        in_specs=[pl.BlockSpec((seq, gw), lambda b, g: (b, g)),
                  pl.BlockSpec((1, gw, gw), lambda b, g: (g, 0, 0)),
                  pl.BlockSpec((1, gw), lambda b, g: (0, g))],
        out_specs=pl.BlockSpec((seq, gw), lambda b, g: (b, g)),
        out_shape=jax.ShapeDtypeStruct((t, ng * gw), BF16),
        compiler_params=_cparams("arbitrary", "arbitrary"),
    )(zc, w_pool, pool_scale.reshape(1, ng * gw))


def _sigmoid(x):
    return 1.0 / (1.0 + jnp.exp(-x))


def _merge_kernel(h_ref, wg0, wg1, wg2, bg0, bg1, bg2, a_ref, b_ref, c_ref, wa, wb, wc, o_ref):
    h = h_ref[...]
    acc = None
    for wg, bg, br, wbr in ((wg0, bg0, a_ref, wa), (wg1, bg1, b_ref, wb), (wg2, bg2, c_ref, wc)):
        gate = _sigmoid(jnp.dot(h, wg[...], preferred_element_type=F32) + bg[...])
        term = gate * jnp.dot(br[...], wbr[...], preferred_element_type=F32)
        acc = term if acc is None else acc + term
    o_ref[...] = acc.astype(o_ref.dtype)


def _merge(h, w_gate, b_gate, a_out, b_out, c_out, w_a, w_b, w_c):
    t, d = h.shape
    tm = min(t, 512)
    tn = 256
    nd = d // tn
    rows = lambda width: pl.BlockSpec((tm, width), lambda i, j: (i, 0))
    cols = lambda k, off: pl.BlockSpec((k, tn), lambda i, j: (0, off * nd + j))
    in_specs = ([rows(d)] + [cols(d, n) for n in range(3)] + [cols(1, n) for n in range(3)]
                + [rows(a_out.shape[1]), rows(b_out.shape[1]), rows(c_out.shape[1])]
                + [cols(a_out.shape[1], 0), cols(b_out.shape[1], 0), cols(c_out.shape[1], 0)])
    bg = b_gate.reshape(1, 3 * d)
    return pl.pallas_call(
        _merge_kernel,
        grid=(t // tm, nd),
        in_specs=in_specs,
        out_specs=pl.BlockSpec((tm, tn), lambda i, j: (i, j)),
        out_shape=jax.ShapeDtypeStruct((t, d), BF16),
        compiler_params=_cparams("arbitrary", "arbitrary"),
    )(h, w_gate, w_gate, w_gate, bg, bg, bg, a_out, b_out, c_out, w_a, w_b, w_c)


def _outproj_kernel(m_ref, w_ref, x_ref, o_ref):
    o_ref[...] = x_ref[...] + jnp.dot(m_ref[...], w_ref[...], preferred_element_type=F32)


def _outproj(merged, w_out, x):
    t, d = x.shape
    tm = min(t, 512)
    tn = 512
    return pl.pallas_call(
        _outproj_kernel,
        grid=(t // tm, d // tn),
        in_specs=[pl.BlockSpec((tm, d), lambda i, j: (i, 0)),
                  pl.BlockSpec((d, tn), lambda i, j: (0, j)),
                  pl.BlockSpec((tm, tn), lambda i, j: (i, j))],
        out_specs=pl.BlockSpec((tm, tn), lambda i, j: (i, j)),
        out_shape=jax.ShapeDtypeStruct((t, d), F32),
        compiler_params=_cparams("arbitrary", "arbitrary"),
    )(merged, w_out, x)


GROUP_LANE0 = N_EXPERTS


def _lane_min_index(mask, lane):
    return jnp.min(jnp.where(mask, lane, LANES), axis=1, keepdims=True)


def _router_kernel(x_ref, g_ref, wr_ref, br_ref, hp_ref, info_ref, cnt_ref, carry_ref):
    @pl.when(pl.program_id(0) == 0)
    def _():
        carry_ref[...] = jnp.zeros_like(carry_ref)

    hb = _rms(x_ref[...], g_ref[...]).astype(BF16)
    tm, d = hb.shape
    lo = pltpu.bitcast(hb[:, :d // 2].astype(F32), U32)
    hi = pltpu.bitcast(hb[:, d // 2:].astype(F32), U32)
    hp_ref[...] = hi | (lo >> 16)

    logits = jnp.dot(hb, wr_ref[...], preferred_element_type=F32) + br_ref[...]
    lane = lax.broadcasted_iota(I32, logits.shape, 1)
    is_group = (lane >= GROUP_LANE0) & (lane < GROUP_LANE0 + N_EXPERT_GROUPS)
    lg = jnp.where(is_group, logits, NEG_INF)
    gmax = jnp.max(lg, axis=1, keepdims=True)
    gsel = _lane_min_index(lg == gmax, lane) - GROUP_LANE0
    pg_sel = 1.0 / jnp.sum(jnp.where(is_group, jnp.exp(lg - gmax), 0.0), axis=1, keepdims=True)
    in_group = (lane < N_EXPERTS) & ((lane // EXPERTS_PER_GROUP) == gsel)
    le = jnp.where(in_group, logits, NEG_INF)
    ee = jnp.where(in_group, jnp.exp(le - jnp.max(le, axis=1, keepdims=True)), 0.0)
    pe = jnp.where(in_group, ee / jnp.sum(ee, axis=1, keepdims=True), -1.0)
    v0 = jnp.max(pe, axis=1, keepdims=True)
    e0 = _lane_min_index(pe == v0, lane)
    pe1 = jnp.where(lane == e0, -1.0, pe)
    v1 = jnp.max(pe1, axis=1, keepdims=True)
    e1 = _lane_min_index(pe1 == v1, lane)
    w0 = pg_sel * v0 / (v0 + v1)
    w1 = pg_sel * v1 / (v0 + v1)
    onehot = jnp.where((lane == e0) | (lane == e1), 1.0, 0.0)
    r = lax.broadcasted_iota(I32, (tm, tm), 0)
    c = lax.broadcasted_iota(I32, (tm, tm), 1)
    before = jnp.where(c < r, 1.0, 0.0).astype(BF16)
    prior = jnp.dot(before, onehot.astype(BF16), preferred_element_type=F32) + carry_ref[...]
    r0 = jnp.sum(jnp.where(lane == e0, prior, 0.0), axis=1, keepdims=True)
    r1 = jnp.sum(jnp.where(lane == e1, prior, 0.0), axis=1, keepdims=True)
    carry_ref[...] = carry_ref[...] + jnp.sum(onehot, axis=0, keepdims=True)
    cnt_ref[...] = carry_ref[...]
    info = jnp.zeros_like(logits)
    for k, val in enumerate((e0.astype(F32), e1.astype(F32), r0, r1, w0, w1)):
        info = jnp.where(lane == k, val, info)
    info_ref[...] = info


def _router(x, g, w_router, b_router):
    t, d = x.shape
    tm = min(t, 256)
    return pl.pallas_call(
        _router_kernel,
        grid=(t // tm,),
        in_specs=[pl.BlockSpec((tm, d), lambda i: (i, 0)),
                  pl.BlockSpec((1, d), lambda i: (0, 0)),
                  pl.BlockSpec((d, LANES), lambda i: (0, 0)),
                  pl.BlockSpec((1, LANES), lambda i: (0, 0))],
        out_specs=[pl.BlockSpec((tm, d // 2), lambda i: (i, 0)),
                   pl.BlockSpec((tm, LANES), lambda i: (i, 0)),
                   pl.BlockSpec((1, LANES), lambda i: (0, 0))],
        out_shape=[jax.ShapeDtypeStruct((t, d // 2), U32),
                   jax.ShapeDtypeStruct((t, LANES), F32),
                   jax.ShapeDtypeStruct((1, LANES), F32)],
        scratch_shapes=[pltpu.VMEM((1, LANES), F32)],
        compiler_params=_cparams("arbitrary"),
    )(x, g.reshape(1, d), w_router, b_router)


def _plan_kernel(cnt_ref, info_ref, dest_ref, meta_ref):
    lane = lax.broadcasted_iota(I32, (8, LANES), 1)
    cnt = jnp.broadcast_to(cnt_ref[...], (8, LANES))
    nblk = jnp.floor((cnt + (MOE_BLOCK - 1)) * (1.0 / MOE_BLOCK))
    r = lax.broadcasted_iota(I32, (LANES, LANES), 0)
    c = lax.broadcasted_iota(I32, (LANES, LANES), 1)
    upto = jnp.where(r <= c, 1.0, 0.0).astype(BF16)
    end_blk = jnp.dot(nblk.astype(BF16), upto, preferred_element_type=F32)
    start_row = (end_blk - nblk) * MOE_BLOCK

    info = info_ref[...]
    tl = lax.broadcasted_iota(I32, info.shape, 1)
    e0, e1, r0, r1 = (info[:, k:k + 1] for k in range(4))
    start = start_row[:1, :]
    d0 = jnp.sum(jnp.where(tl == e0.astype(I32), start, 0.0), axis=1, keepdims=True) + r0
    d1 = jnp.sum(jnp.where(tl == e1.astype(I32), start, 0.0), axis=1, keepdims=True) + r1
    dest_ref[...] = jnp.where(tl == 0, d0, jnp.where(tl == 1, d1, 0.0)).astype(I32)

    @pl.when(pl.program_id(0) == 0)
    def _():
        nb = meta_ref.shape[0]
        blk = lax.broadcasted_iota(I32, (nb, LANES), 0).astype(F32)
        ml = lax.broadcasted_iota(I32, (nb, LANES), 1)
        done = jnp.where((ml < N_EXPERTS) & (end_blk[:1, :] <= blk), 1.0, 0.0)
        block_e = jnp.minimum(jnp.sum(done, axis=1, keepdims=True), N_EXPERTS - 1.0)
        n_used = jnp.sum(jnp.where(lane[:1, :] == N_EXPERTS - 1, end_blk[:1, :], 0.0), axis=1, keepdims=True)
        meta_ref[...] = jnp.where(ml == 0, block_e, jnp.where(ml == 1, n_used, 0.0)).astype(I32)


def _plan(counts, info, n_blocks):
    t = info.shape[0]
    tm = min(t, 1024)
    nb = -(-n_blocks // 8) * 8
    return pl.pallas_call(
        _plan_kernel,
        grid=(t // tm,),
        in_specs=[pl.BlockSpec((1, LANES), lambda i: (0, 0)), pl.BlockSpec((tm, LANES), lambda i: (i, 0))],
        out_specs=[pl.BlockSpec((tm, LANES), lambda i: (i, 0)), pl.BlockSpec((nb, LANES), lambda i: (0, 0))],
        out_shape=[jax.ShapeDtypeStruct((t, LANES), I32), jax.ShapeDtypeStruct((nb, LANES), I32)],
        compiler_params=_cparams("arbitrary"),
    )(counts, info)


def _row_copy(src, src_row, dst, dst_row, n, sem):
    return pltpu.make_async_copy(src.at[pl.ds(src_row, n)], dst.at[pl.ds(dst_row, n)], sem)


def _dispatch_kernel(dest_ref, hp_ref, xs_in_ref, xs_ref, sem, *, tokens_per_step):
    del xs_in_ref
    base = pl.program_id(0) * tokens_per_step

    def issue(j, carry):
        tok = base + j
        _row_copy(hp_ref, j, xs_ref, dest_ref[2 * tok], 1, sem).start()
        _row_copy(hp_ref, j, xs_ref, dest_ref[2 * tok + 1], 1, sem).start()
        return carry

    lax.fori_loop(0, tokens_per_step, issue, 0)
    _row_copy(hp_ref, 0, xs_ref, 0, tokens_per_step, sem).wait()
    _row_copy(hp_ref, 0, xs_ref, 0, tokens_per_step, sem).wait()


def _dispatch(dest, hp, n_rows):
    t, wd = hp.shape
    tc = min(t, 512)
    xs0 = jnp.zeros((n_rows, wd), U32)
    return pl.pallas_call(
        functools.partial(_dispatch_kernel, tokens_per_step=tc),
        grid_spec=pltpu.PrefetchScalarGridSpec(
            num_scalar_prefetch=1,
            grid=(t // tc,),
            in_specs=[pl.BlockSpec((tc, wd), lambda i, dr: (i, 0)), pl.BlockSpec(memory_space=pl.ANY)],
            out_specs=pl.BlockSpec(memory_space=pl.ANY),
            scratch_shapes=[pltpu.SemaphoreType.DMA(())],
        ),
        out_shape=jax.ShapeDtypeStruct((n_rows, wd), U32),
        input_output_aliases={2: 0},
        compiler_params=_cparams("arbitrary"),
    )(dest, hp, xs0)


def _expert_kernel(be_ref, nu_ref, xs_ref, wg_ref, wu_ref, wd_ref, o_ref):
    del be_ref

    @pl.when(pl.program_id(0) < nu_ref[0])
    def _():
        u = xs_ref[...]
        half = u.shape[1]
        lo = pltpu.bitcast(u << 16, F32).astype(BF16)
        hi = pltpu.bitcast(u & jnp.uint32(0xFFFF0000), F32).astype(BF16)

        def proj(w_ref):
            return (jnp.dot(lo, w_ref[0, :half, :], preferred_element_type=F32)
                    + jnp.dot(hi, w_ref[0, half:, :], preferred_element_type=F32))

        gate = proj(wg_ref)
        hid = gate * _sigmoid(gate) * proj(wu_ref)
        o_ref[...] = jnp.dot(hid.astype(BF16), wd_ref[0], preferred_element_type=F32)

    @pl.when(pl.program_id(0) >= nu_ref[0])
    def _():
        o_ref[...] = jnp.zeros_like(o_ref)


def _experts(block_e, n_used, xs, w_e_gate, w_e_up, w_e_down):
    n_rows, half = xs.shape
    d = 2 * half
    hid = w_e_gate.shape[2]
    n_blocks = n_rows // MOE_BLOCK
    blk = lambda i, be, nu: jnp.minimum(i, nu[0] - 1)
    return pl.pallas_call(
        _expert_kernel,
        grid_spec=pltpu.PrefetchScalarGridSpec(
            num_scalar_prefetch=2,
            grid=(n_blocks,),
            in_specs=[pl.BlockSpec((MOE_BLOCK, half), lambda i, be, nu: (blk(i, be, nu), 0)),
                      pl.BlockSpec((1, d, hid), lambda i, be, nu: (be[blk(i, be, nu)], 0, 0)),
                      pl.BlockSpec((1, d, hid), lambda i, be, nu: (be[blk(i, be, nu)], 0, 0)),
                      pl.BlockSpec((1, hid, d), lambda i, be, nu: (be[blk(i, be, nu)], 0, 0))],
            out_specs=pl.BlockSpec((MOE_BLOCK, d), lambda i, be, nu: (i, 0)),
        ),
        out_shape=jax.ShapeDtypeStruct((n_rows, d), F32),
        compiler_params=_cparams("arbitrary"),
    )(block_e, n_used, xs, w_e_gate, w_e_up, w_e_down)


def _combine_kernel(dest_ref, x_ref, info_ref, g_ref, y_ref, *rest, tokens_per_step, final):
    if final:
        o_ref, buf, sem = rest
    else:
        x2_ref, o_ref, buf, sem = rest
    base = pl.program_id(0) * tokens_per_step

    def issue(j, carry):
        tok = base + j
        _row_copy(y_ref, dest_ref[2 * tok], buf.at[0], j, 1, sem).start()
        _row_copy(y_ref, dest_ref[2 * tok + 1], buf.at[1], j, 1, sem).start()
        return carry

    lax.fori_loop(0, tokens_per_step, issue, 0)
    _row_copy(y_ref, 0, buf.at[0], 0, tokens_per_step, sem).wait()
    _row_copy(y_ref, 0, buf.at[1], 0, tokens_per_step, sem).wait()
    info = info_ref[...]
    x2 = x_ref[...] + (buf[0] * info[:, 4:5] + buf[1] * info[:, 5:6])
    if not final:
        x2_ref[...] = x2
    o_ref[...] = _rms(x2, g_ref[...]).astype(o_ref.dtype)


def _combine(dest, x, info, g_next, y_rows, final):
    t, d = x.shape
    tc = min(t, 128)
    tile = pl.BlockSpec((tc, d), lambda i, dr: (i, 0))
    out_specs = [tile] if final else [tile, tile]
    out_shape = ([jax.ShapeDtypeStruct((t, d), F32)] if final
                 else [jax.ShapeDtypeStruct((t, d), F32), jax.ShapeDtypeStruct((t, d), BF16)])
    return pl.pallas_call(
        functools.partial(_combine_kernel, tokens_per_step=tc, final=final),
        grid_spec=pltpu.PrefetchScalarGridSpec(
            num_scalar_prefetch=1,
            grid=(t // tc,),
            in_specs=[tile,
                      pl.BlockSpec((tc, LANES), lambda i, dr: (i, 0)),
                      pl.BlockSpec((1, d), lambda i, dr: (0, 0)),
                      pl.BlockSpec(memory_space=pl.ANY)],
            out_specs=out_specs,
            scratch_shapes=[pltpu.VMEM((2, tc, d), F32), pltpu.SemaphoreType.DMA(())],
        ),
        out_shape=out_shape,
        compiler_params=_cparams("arbitrary"),
    )(dest, x, info, g_next.reshape(1, d), y_rows)


def _pad_cols(w, width):
    return jnp.pad(w, ((0, 0), (0, width - w.shape[1])))


def _split_w_in(w_in, d):
    dsa_heads = d // (4 * HEAD_DIM)
    hpg = d // 1024
    dil_heads = len(DIL_PATTERNS) * hpg
    a_w, b_w = dsa_heads * HEAD_DIM, dil_heads * HEAD_DIM
    pool_w = d - a_w - b_w
    sizes = (a_w, HEAD_DIM, HEAD_DIM, IDX_HEADS * IDX_DIM, IDX_DIM, IDX_HEADS, b_w, b_w, b_w, pool_w)
    offs = [0]
    for s in sizes:
        offs.append(offs[-1] + s)
    qa, ka, va, qi, ki, wi, qb, kb, vb, c = (w_in[:, offs[n]:offs[n + 1]] for n in range(len(sizes)))
    zero64 = jnp.zeros_like(ki)
    mxu_pad = lambda w: _pad_cols(w, -(-w.shape[1] // MXU_WIDTH) * MXU_WIDTH)
    w_rope = mxu_pad(jnp.concatenate([qa, ka, qb, kb], axis=1))
    w_idx = mxu_pad(jnp.concatenate([qi, ki, zero64, zero64, ki], axis=1))
    w_val = mxu_pad(jnp.concatenate([va, vb], axis=1))
    w_c = mxu_pad(jnp.concatenate([c, wi], axis=1))
    lay = dict(dsa_heads=dsa_heads, dil_heads_per_group=hpg,
               ka=a_w, qb=a_w + HEAD_DIM, kb=a_w + HEAD_DIM + b_w,
               ki_lo=IDX_HEADS * IDX_DIM, ki_hi=IDX_HEADS * IDX_DIM + LANES,
               va=0, vb=HEAD_DIM, wi=pool_w)
    return tuple(w.astype(BF16) for w in (w_rope, w_idx, w_val, w_c)), lay


def _col_tile(n, cap=512):
    best = LANES
    for m in range(1, cap // LANES + 1):
        if n % (m * LANES) == 0:
            best = m * LANES
    return best


def kernel(x, positions, norm1_g, w_in, w_gate, b_gate, w_br_a, w_br_b, w_br_c, w_pool, pool_scale,
           w_out, norm2_g, w_rg, b_rg, w_re, b_re, w_e_gate, w_e_up, w_e_down, final_g):
    bsz, seq, d = x.shape
    depth = w_in.shape[0]
    t = bsz * seq
    n_blocks = -(-(2 * t) // MOE_BLOCK) + N_EXPERTS
    n_rows = n_blocks * MOE_BLOCK

    c128, s128, c64, s64 = _rope_tables(positions)
    xt = x.reshape(t, d)
    h = _rms_norm(xt, norm1_g[0], BF16)
    out = None
    for l in range(depth):
        (w_rope, w_idx, w_val, w_c), lay = _split_w_in(w_in[l], d)
        zr = _project(h, w_rope, BF16, _col_tile(w_rope.shape[1]), rope=(HEAD_DIM, c128, s128))
        zi = _project(h, w_idx, BF16, _col_tile(w_idx.shape[1]), rope=(IDX_DIM, c64, s64))
        zv = _project(h, w_val, BF16, _col_tile(w_val.shape[1]))
        zc = _project(h, w_c, F32, _col_tile(w_c.shape[1]))

        a_out = _dsa(zr, zi, zv, zc, bsz, seq, lay)
        b_out = _dilated(zr, zv, bsz, seq, lay)
        c_out = _pool(zc, w_pool[l].astype(BF16), pool_scale[l], bsz, seq)

        merged = _merge(h, w_gate[l].astype(BF16), b_gate[l], a_out, b_out, c_out,
                        w_br_a[l].astype(BF16), w_br_b[l].astype(BF16), w_br_c[l].astype(BF16))
        x1 = _outproj(merged, w_out[l].astype(BF16), xt)

        w_router = _pad_cols(jnp.concatenate([w_re[l], w_rg[l]], axis=1), LANES).astype(BF16)
        b_router = _pad_cols(jnp.concatenate([b_re[l], b_rg[l]])[None, :], LANES)
        hp, info, counts = _router(x1, norm2_g[l], w_router, b_router)
        dest2, meta = _plan(counts, info, n_blocks)
        dest = dest2[:, :2].reshape(2 * t)
        block_e = meta[:n_blocks, 0]
        n_used = meta[:1, 1]
        xs = _dispatch(dest, hp, n_rows)
        y_rows = _experts(block_e, n_used, xs, w_e_gate[l].astype(BF16), w_e_up[l].astype(BF16),
                          w_e_down[l].astype(BF16))
        final = l == depth - 1
        g_next = final_g if final else norm1_g[l + 1]
        res = _combine(dest, x1, info, g_next, y_rows, final)
        if final:
            out = res[0]
        else:
            xt, h = res
    return out.reshape(bsz, seq, d)
```

```python
import functools

import jax
import jax.numpy as jnp
from jax import lax
from jax.experimental import pallas as pl
from jax.experimental.pallas import tpu as pltpu

F32 = jnp.float32
BF16 = jnp.bfloat16
I32 = jnp.int32
U32 = jnp.uint32

LANES = 128
MXU_WIDTH = 256
VMEM_LIMIT = 56 * 1024 * 1024

HEAD_DIM = 128
ROPE_THETA = 500000.0
ROPE_FRACTION = 4
NORM_EPS = 1e-6
IDX_HEADS = 16
IDX_DIM = 64
DSA_TOPK = 256
Q_BLOCK = 128
DIL_PATTERNS = ((128, 1), (512, 4), (2048, 16))
POOL_WINDOWS = (2, 4, 8, 16)
N_EXPERT_GROUPS = 4
EXPERTS_PER_GROUP = 8
N_EXPERTS = N_EXPERT_GROUPS * EXPERTS_PER_GROUP
MOE_BLOCK = 256
NEG_INF = float("-inf")


def _cparams(*sem):
    return pltpu.CompilerParams(dimension_semantics=sem, vmem_limit_bytes=VMEM_LIMIT)


def _rope_table_kernel(pos_ref, f128_ref, s128_ref, f64_ref, s64_ref, c128, sn128, c64, sn64):
    p = pos_ref[...].astype(F32)
    a = p * f128_ref[...]
    c128[...] = jnp.cos(a)
    sn128[...] = jnp.sin(a) * s128_ref[...]
    a = p * f64_ref[...]
    c64[...] = jnp.cos(a)
    sn64[...] = jnp.sin(a) * s64_ref[...]


def _rope_lane_patterns():
    lane = jnp.arange(LANES)

    def pattern(width):
        rd = width // ROPE_FRACTION
        half = rd // 2
        inv = ROPE_THETA ** (-jnp.arange(half, dtype=F32) * (2.0 / rd))
        l = lane % width
        freq = jnp.where(l < rd, inv[l % half], 0.0).astype(F32)
        sign = jnp.where(l < half, -1.0, jnp.where(l < rd, 1.0, 0.0)).astype(F32)
        return freq[None, :], sign[None, :]

    return pattern(HEAD_DIM) + pattern(IDX_DIM)


def _rope_tables(positions):
    t = positions.size
    tm = min(t, 1024)
    f128, s128, f64, s64 = _rope_lane_patterns()
    vec = pl.BlockSpec((1, LANES), lambda i: (0, 0))
    tab = pl.BlockSpec((tm, LANES), lambda i: (i, 0))
    return pl.pallas_call(
        _rope_table_kernel,
        grid=(t // tm,),
        in_specs=[pl.BlockSpec((tm, 1), lambda i: (i, 0)), vec, vec, vec, vec],
        out_specs=[tab] * 4,
        out_shape=[jax.ShapeDtypeStruct((t, LANES), F32)] * 4,
        compiler_params=_cparams("arbitrary"),
    )(positions.reshape(t, 1), f128, s128, f64, s64)


def _rms(x, g):
    ms = jnp.mean(x * x, axis=-1, keepdims=True)
    return x * lax.rsqrt(ms + NORM_EPS) * g


def _norm_kernel(x_ref, g_ref, o_ref):
    o_ref[...] = _rms(x_ref[...], g_ref[...]).astype(o_ref.dtype)


def _rms_norm(x, g, out_dtype):
    t, d = x.shape
    tm = min(t, 512)
    return pl.pallas_call(
        _norm_kernel,
        grid=(t // tm,),
        in_specs=[pl.BlockSpec((tm, d), lambda i: (i, 0)), pl.BlockSpec((1, d), lambda i: (0, 0))],
        out_specs=pl.BlockSpec((tm, d), lambda i: (i, 0)),
        out_shape=jax.ShapeDtypeStruct((t, d), out_dtype),
        compiler_params=_cparams("arbitrary"),
    )(x, g.reshape(1, d))


def _proj_kernel(x_ref, w_ref, *rest, rope_width):
    acc = jnp.dot(x_ref[...], w_ref[...], preferred_element_type=F32)
    if rope_width is None:
        (o_ref,) = rest
        o_ref[...] = acc.astype(o_ref.dtype)
        return
    c_ref, s_ref, o_ref = rest
    half = rope_width // ROPE_FRACTION // 2
    c = c_ref[...]
    s = s_ref[...]
    lane = lax.broadcasted_iota(I32, c.shape, 1)
    first_half = (lane % rope_width) < half
    for g in range(acc.shape[1] // LANES):
        z = acc[:, g * LANES:(g + 1) * LANES]
        partner = jnp.where(first_half, pltpu.roll(z, LANES - half, 1), pltpu.roll(z, half, 1))
        o_ref[:, g * LANES:(g + 1) * LANES] = (z * c + partner * s).astype(o_ref.dtype)


def _project(h, w, out_dtype, tn, rope=None):
    t, k = h.shape
    n = w.shape[1]
    tm = min(t, 2048)
    in_specs = [pl.BlockSpec((tm, k), lambda i, j: (i, 0)), pl.BlockSpec((k, tn), lambda i, j: (0, j))]
    args = [h, w]
    if rope is not None:
        tab = pl.BlockSpec((tm, LANES), lambda i, j: (i, 0))
        in_specs += [tab, tab]
        args += [rope[1], rope[2]]
    return pl.pallas_call(
        functools.partial(_proj_kernel, rope_width=None if rope is None else rope[0]),
        grid=(t // tm, n // tn),
        in_specs=in_specs,
        out_specs=pl.BlockSpec((tm, tn), lambda i, j: (i, j)),
        out_shape=jax.ShapeDtypeStruct((t, n), out_dtype),
        compiler_params=_cparams("arbitrary", "arbitrary"),
    )(*args)


KEY_CHUNK = 256
SUBLANES = 8


def _sublane_allreduce(x, op):
    for shift in (4, 2, 1):
        x = op(x, pltpu.roll(x, shift, 0))
    return x


def _tree_reduce(x, op):
    while x.shape[0] > 1:
        half = x.shape[0] // 2
        x = op(x[:half], x[half:])
    return x[0]


def _vregs(x):
    return x.reshape(x.shape[0] // SUBLANES, SUBLANES, LANES)


def _dsa_kernel(qa_ref, ka_ref, va_ref, qi_ref, kil_ref, kih_ref, wi_ref, o_ref,
                key_ref, vat_ref, acc_ref, m_ref, l_ref, *, topk, n_heads):
    nq = o_ref.shape[0]
    ck = KEY_CHUNK
    i = pl.program_id(1)
    t0 = i * nq
    n_chunks = (t0 + nq + ck - 1) // ck
    nt = (((1,), (1,)), ((), ()))

    @pl.when(i == 0)
    def _():
        for c in range(vat_ref.shape[0]):
            vat_ref[c] = va_ref[c * ck:(c + 1) * ck, :].astype(F32).T.astype(BF16)

    def causal_mask(c):
        shape = (ck // SUBLANES, SUBLANES, nq)
        spos = (c * ck + lax.broadcasted_iota(I32, shape, 0) * SUBLANES
                + lax.broadcasted_iota(I32, shape, 1))
        tpos = t0 + lax.broadcasted_iota(I32, shape, 2)
        return spos <= tpos

    wt = (wi_ref[...] * ((IDX_DIM ** -0.5) * (IDX_HEADS ** -0.5))).T
    w8 = [jnp.broadcast_to(wt[h:h + 1, :], (SUBLANES, LANES)) for h in range(IDX_HEADS)]
    q_stack = [jnp.concatenate([qi_ref[:, (2 * m) * LANES:(2 * m + 1) * LANES],
                                qi_ref[:, (2 * m + 1) * LANES:(2 * m + 2) * LANES]], axis=0)
               for m in range(IDX_HEADS // 4)]

    def score_chunk(c, carry):
        r0 = pl.multiple_of(c * ck, ck)
        kil = kil_ref[pl.ds(r0, ck), :]
        kih = kih_ref[pl.ds(r0, ck), :]
        acc = jnp.zeros((ck // SUBLANES, SUBLANES, LANES), F32)
        for m in range(IDX_HEADS // 4):
            d_even = lax.dot_general(kil, q_stack[m], nt, preferred_element_type=F32)
            d_odd = lax.dot_general(kih, q_stack[m], nt, preferred_element_type=F32)
            for dots, h in ((d_even[:, :nq], 4 * m), (d_odd[:, :nq], 4 * m + 1),
                            (d_even[:, nq:], 4 * m + 2), (d_odd[:, nq:], 4 * m + 3)):
                acc = acc + jnp.maximum(_vregs(dots), 0.0) * w8[h][None]
        score = jnp.where(causal_mask(c), acc, NEG_INF).reshape(ck, nq)
        bits = pltpu.bitcast(score, I32)
        key_ref[c] = bits ^ ((bits >> 31) & jnp.int32(0x7FFFFFFF))
        return carry

    lax.fori_loop(0, n_chunks, score_chunk, 0)

    def count_ge(cand):
        def body(c, cnt):
            return cnt + _tree_reduce(jnp.where(_vregs(key_ref[c]) >= cand[None], 1.0, 0.0), jnp.add)
        cnt = lax.fori_loop(0, n_chunks, body, jnp.zeros((SUBLANES, LANES), F32))
        return _sublane_allreduce(cnt, jnp.add)

    int_min = jnp.int32(-2 ** 31)
    thr = jnp.where(count_ge(jnp.zeros((SUBLANES, LANES), I32)) >= topk, jnp.int32(0), int_min)

    def refine(it, thr):
        cand = thr | (jnp.int32(1) << (30 - it))
        return jnp.where(count_ge(cand) >= topk, cand, thr)

    thr = lax.fori_loop(0, 31, refine, thr)

    m_ref[...] = jnp.full(m_ref.shape, NEG_INF, F32)
    l_ref[...] = jnp.zeros(l_ref.shape, F32)
    acc_ref[...] = jnp.zeros(acc_ref.shape, F32)

    def attend(c, carry):
        r0 = pl.multiple_of(c * ck, ck)
        keep = (_vregs(key_ref[c]) >= thr[None]) & causal_mask(c)
        bias = jnp.where(keep, 0.0, NEG_INF)
        ka = ka_ref[pl.ds(r0, ck), :]
        vat = vat_ref[c]
        for h in range(n_heads):
            q = qa_ref[:, h * HEAD_DIM:(h + 1) * HEAD_DIM]
            s = _vregs(lax.dot_general(ka, q, nt, preferred_element_type=F32) * (HEAD_DIM ** -0.5)) + bias
            m_old = m_ref[h]
            m_new = jnp.maximum(m_old, _sublane_allreduce(_tree_reduce(s, jnp.maximum), jnp.maximum))
            m_safe = jnp.where(m_new == NEG_INF, 0.0, m_new)
            alpha = jnp.exp(m_old - m_safe)
            p = jnp.exp(s - m_safe[None])
            l_ref[h] = alpha * l_ref[h] + _tree_reduce(p, jnp.add)
            pv = jnp.dot(vat, p.reshape(ck, nq).astype(BF16), preferred_element_type=F32)
            acc_ref[h] = (_vregs(acc_ref[h]) * alpha[None]).reshape(HEAD_DIM, nq) + pv
            m_ref[h] = m_new
        return carry

    lax.fori_loop(0, n_chunks, attend, 0)
    for h in range(n_heads):
        l = _sublane_allreduce(l_ref[h], jnp.add)
        o = (_vregs(acc_ref[h]) / l[None]).reshape(HEAD_DIM, nq)
        o_ref[:, h * HEAD_DIM:(h + 1) * HEAD_DIM] = o.T.astype(o_ref.dtype)


def _dsa(zr, zi, zv, zc, bsz, seq, lay):
    t = bsz * seq
    nqb = seq // Q_BLOCK
    n_heads = lay["dsa_heads"]
    a_w = n_heads * HEAD_DIM
    qw = IDX_HEADS * IDX_DIM
    nck = seq // KEY_CHUNK
    row = lambda b, i: b * nqb + i
    in_specs = [
        pl.BlockSpec((Q_BLOCK, a_w), lambda b, i: (row(b, i), 0)),
        pl.BlockSpec((seq, HEAD_DIM), lambda b, i: (b, lay["ka"] // HEAD_DIM)),
        pl.BlockSpec((seq, HEAD_DIM), lambda b, i: (b, lay["va"] // HEAD_DIM)),
        pl.BlockSpec((Q_BLOCK, qw), lambda b, i: (row(b, i), 0)),
        pl.BlockSpec((seq, LANES), lambda b, i: (b, lay["ki_lo"] // LANES)),
        pl.BlockSpec((seq, LANES), lambda b, i: (b, lay["ki_hi"] // LANES)),
        pl.BlockSpec((Q_BLOCK, LANES), lambda b, i: (row(b, i), lay["wi"] // LANES)),
    ]
    return pl.pallas_call(
        functools.partial(_dsa_kernel, topk=min(DSA_TOPK, seq // 4), n_heads=n_heads),
        grid=(bsz, nqb),
        in_specs=in_specs,
        out_specs=pl.BlockSpec((Q_BLOCK, a_w), lambda b, i: (row(b, i), 0)),
        out_shape=jax.ShapeDtypeStruct((t, a_w), BF16),
        scratch_shapes=[pltpu.VMEM((nck, KEY_CHUNK, Q_BLOCK), I32),
                        pltpu.VMEM((nck, HEAD_DIM, KEY_CHUNK), BF16),
                        pltpu.VMEM((n_heads, HEAD_DIM, Q_BLOCK), F32),
                        pltpu.VMEM((n_heads, SUBLANES, Q_BLOCK), F32),
                        pltpu.VMEM((n_heads, SUBLANES, Q_BLOCK), F32)],
        compiler_params=_cparams("arbitrary", "arbitrary"),
    )(zr, zr, zv, zi, zi, zi, zc)


MIX_ROWS = 256


def _band_attention(q, k_blocks, v_blocks):
    bs = q.shape[0]
    nt = (((1,), (1,)), ((), ()))
    k = k_blocks[0] if len(k_blocks) == 1 else jnp.concatenate(k_blocks, axis=0)
    v = v_blocks[0] if len(v_blocks) == 1 else jnp.concatenate(v_blocks, axis=0)
    s = lax.dot_general(q, k, nt, preferred_element_type=F32) * (HEAD_DIM ** -0.5)
    a = lax.broadcasted_iota(I32, s.shape, 0)
    c = lax.broadcasted_iota(I32, s.shape, 1)
    if len(k_blocks) == 1:
        valid = c <= a
    else:
        valid = (a + bs - c).astype(U32) <= jnp.uint32(bs)
    s = jnp.where(valid, s, NEG_INF)
    m = jnp.max(s, axis=1, keepdims=True)
    p = jnp.exp(s - m)
    l = jnp.sum(p, axis=1, keepdims=True)
    o = jnp.dot(p.astype(BF16), v, preferred_element_type=F32) / l
    return o, m + jnp.log(l)


def _dil_kernel(*refs, seq):
    ng = len(DIL_PATTERNS)
    q_refs, k_refs, v_refs = refs[:ng], refs[ng:2 * ng], refs[2 * ng:3 * ng]
    o_ref, qf, kf, vf, of, lf = refs[3 * ng:]
    for g, (window, dil) in enumerate(DIL_PATTERNS):
        ls = seq // dil
        bs = min(ls, Q_BLOCK)
        assert ls % bs == 0 and window // dil == Q_BLOCK
        if dil > 1:
            qf[...] = q_refs[g][...].astype(F32)
            kf[...] = k_refs[g][...].astype(F32)
            vf[...] = v_refs[g][...].astype(F32)
        for x in range(dil):
            def rows(n):
                return pl.ds(x + dil * bs * n, bs, stride=dil) if dil > 1 else pl.ds(bs * n, bs)

            def load(src_bf16, src_f32, n):
                return src_bf16[rows(n), :] if dil == 1 else src_f32[rows(n), :].astype(BF16)

            k_prev = v_prev = None
            for n in range(ls // bs):
                q = load(q_refs[g], qf, n)
                k_cur = load(k_refs[g], kf, n)
                v_cur = load(v_refs[g], vf, n)
                if n == 0:
                    o, lse = _band_attention(q, [k_cur], [v_cur])
                else:
                    o, lse = _band_attention(q, [k_prev, k_cur], [v_prev, v_cur])
                of[g, rows(n), :] = o
                lf[g, rows(n), :] = jnp.broadcast_to(lse, o.shape)
                k_prev, v_prev = k_cur, v_cur
    mr = min(MIX_ROWS, seq)
    for r in range(seq // mr):
        sl = pl.ds(r * mr, mr)
        lses = [lf[g, sl, :] for g in range(ng)]
        mx = functools.reduce(jnp.maximum, lses)
        es = [jnp.exp(x - mx) for x in lses]
        tot = functools.reduce(lambda u, w: u + w, es)
        acc = functools.reduce(lambda u, w: u + w, [(e / tot) * of[g, sl, :] for g, e in enumerate(es)])
        o_ref[sl, :] = acc.astype(o_ref.dtype)


def _dilated(zr, zv, bsz, seq, lay):
    t = bsz * seq
    hpg = lay["dil_heads_per_group"]
    ng = len(DIL_PATTERNS)

    def head_spec(base, g):
        return pl.BlockSpec((seq, HEAD_DIM), lambda b, s: (b, base // HEAD_DIM + g * hpg + s))

    in_specs = ([head_spec(lay["qb"], g) for g in range(ng)]
                + [head_spec(lay["kb"], g) for g in range(ng)]
                + [head_spec(lay["vb"], g) for g in range(ng)])
    return pl.pallas_call(
        functools.partial(_dil_kernel, seq=seq),
        grid=(bsz, hpg),
        in_specs=in_specs,
        out_specs=pl.BlockSpec((seq, HEAD_DIM), lambda b, s: (b, s)),
        out_shape=jax.ShapeDtypeStruct((t, hpg * HEAD_DIM), BF16),
        scratch_shapes=[pltpu.VMEM((seq, HEAD_DIM), F32)] * 3
                       + [pltpu.VMEM((ng, seq, HEAD_DIM), F32)] * 2,
        compiler_params=_cparams("arbitrary", "arbitrary"),
    )(*([zr] * (2 * ng) + [zv] * ng))


def _pool_kernel(c_ref, wp_ref, ps_ref, o_ref):
    x = c_ref[...]
    row = lax.broadcasted_iota(I32, x.shape, 0)

    def run(window):
        acc = x
        k = 1
        while k < window:
            acc = acc + jnp.where(row >= k, pltpu.roll(acc, k, 0), 0.0)
            k *= 2
        denom = jnp.minimum(row[:, :1] + 1, window).astype(F32)
        y = (acc / denom - x).astype(BF16)
        o = jnp.dot(y, wp_ref[0], preferred_element_type=F32) * ps_ref[...]
        o_ref[...] = o.astype(o_ref.dtype)

    for gi, window in enumerate(POOL_WINDOWS):
        pl.when(pl.program_id(1) == gi)(functools.partial(run, window))


def _pool(zc, w_pool, pool_scale, bsz, seq):
    t = bsz * seq
    ng, gw = w_pool.shape[0], w_pool.shape[1]
    return pl.pallas_call(
        _pool_kernel,
        grid=(bsz, ng),
        in_specs=[pl.BlockSpec((seq, gw), lambda b, g: (b, g)),
                  pl.BlockSpec((1, gw, gw), lambda b, g: (g, 0, 0)),
                  pl.BlockSpec((1, gw), lambda b, g: (0, g))],
        out_specs=pl.BlockSpec((seq, gw), lambda b, g: (b, g)),
        out_shape=jax.ShapeDtypeStruct((t, ng * gw), BF16),
        compiler_params=_cparams("arbitrary", "arbitrary"),
    )(zc, w_pool, pool_scale.reshape(1, ng * gw))


def _sigmoid(x):
    return 1.0 / (1.0 + jnp.exp(-x))


def _merge_kernel(h_ref, wg0, wg1, wg2, bg0, bg1, bg2, a_ref, b_ref, c_ref, wa, wb, wc, o_ref):
    h = h_ref[...]
    acc = None
    for wg, bg, br, wbr in ((wg0, bg0, a_ref, wa), (wg1, bg1, b_ref, wb), (wg2, bg2, c_ref, wc)):
        gate = _sigmoid(jnp.dot(h, wg[...], preferred_element_type=F32) + bg[...])
        term = gate * jnp.dot(br[...], wbr[...], preferred_element_type=F32)
        acc = term if acc is None else acc + term
    o_ref[...] = acc.astype(o_ref.dtype)


def _merge(h, w_gate, b_gate, a_out, b_out, c_out, w_a, w_b, w_c):
    t, d = h.shape
    tm = min(t, 1024)
    tn = 256
    nd = d // tn
    rows = lambda width: pl.BlockSpec((tm, width), lambda i, j: (i, 0))
    cols = lambda k, off: pl.BlockSpec((k, tn), lambda i, j: (0, off * nd + j))
    in_specs = ([rows(d)] + [cols(d, n) for n in range(3)] + [cols(1, n) for n in range(3)]
                + [rows(a_out.shape[1]), rows(b_out.shape[1]), rows(c_out.shape[1])]
                + [cols(a_out.shape[1], 0), cols(b_out.shape[1], 0), cols(c_out.shape[1], 0)])
    bg = b_gate.reshape(1, 3 * d)
    return pl.pallas_call(
        _merge_kernel,
        grid=(t // tm, nd),
        in_specs=in_specs,
        out_specs=pl.BlockSpec((tm, tn), lambda i, j: (i, j)),
        out_shape=jax.ShapeDtypeStruct((t, d), BF16),
        compiler_params=_cparams("arbitrary", "arbitrary"),
    )(h, w_gate, w_gate, w_gate, bg, bg, bg, a_out, b_out, c_out, w_a, w_b, w_c)


def _outproj_kernel(m_ref, w_ref, x_ref, o_ref):
    o_ref[...] = x_ref[...] + jnp.dot(m_ref[...], w_ref[...], preferred_element_type=F32)


def _outproj(merged, w_out, x):
    t, d = x.shape
    tm = min(t, 1024)
    tn = 512
    return pl.pallas_call(
        _outproj_kernel,
        grid=(t // tm, d // tn),
        in_specs=[pl.BlockSpec((tm, d), lambda i, j: (i, 0)),
                  pl.BlockSpec((d, tn), lambda i, j: (0, j)),
                  pl.BlockSpec((tm, tn), lambda i, j: (i, j))],
        out_specs=pl.BlockSpec((tm, tn), lambda i, j: (i, j)),
        out_shape=jax.ShapeDtypeStruct((t, d), F32),
        compiler_params=_cparams("arbitrary", "arbitrary"),
    )(merged, w_out, x)


GROUP_LANE0 = N_EXPERTS


def _lane_min_index(mask, lane):
    return jnp.min(jnp.where(mask, lane, LANES), axis=1, keepdims=True)


def _router_kernel(x_ref, g_ref, wr_ref, br_ref, hp_ref, info_ref, cnt_ref, carry_ref):
    @pl.when(pl.program_id(0) == 0)
    def _():
        carry_ref[...] = jnp.zeros_like(carry_ref)

    hb = _rms(x_ref[...], g_ref[...]).astype(BF16)
    tm, d = hb.shape
    lo = pltpu.bitcast(hb[:, :d // 2].astype(F32), U32)
    hi = pltpu.bitcast(hb[:, d // 2:].astype(F32), U32)
    hp_ref[...] = hi | (lo >> 16)

    logits = jnp.dot(hb, wr_ref[...], preferred_element_type=F32) + br_ref[...]
    lane = lax.broadcasted_iota(I32, logits.shape, 1)
    is_group = (lane >= GROUP_LANE0) & (lane < GROUP_LANE0 + N_EXPERT_GROUPS)
    lg = jnp.where(is_group, logits, NEG_INF)
    gmax = jnp.max(lg, axis=1, keepdims=True)
    gsel = _lane_min_index(lg == gmax, lane) - GROUP_LANE0
    pg_sel = 1.0 / jnp.sum(jnp.where(is_group, jnp.exp(lg - gmax), 0.0), axis=1, keepdims=True)
    in_group = (lane < N_EXPERTS) & ((lane // EXPERTS_PER_GROUP) == gsel)
    le = jnp.where(in_group, logits, NEG_INF)
    ee = jnp.where(in_group, jnp.exp(le - jnp.max(le, axis=1, keepdims=True)), 0.0)
    pe = jnp.where(in_group, ee / jnp.sum(ee, axis=1, keepdims=True), -1.0)
    v0 = jnp.max(pe, axis=1, keepdims=True)
    e0 = _lane_min_index(pe == v0, lane)
    pe1 = jnp.where(lane == e0, -1.0, pe)
    v1 = jnp.max(pe1, axis=1, keepdims=True)
    e1 = _lane_min_index(pe1 == v1, lane)
    w0 = pg_sel * v0 / (v0 + v1)
    w1 = pg_sel * v1 / (v0 + v1)
    onehot = jnp.where((lane == e0) | (lane == e1), 1.0, 0.0)
    r = lax.broadcasted_iota(I32, (tm, tm), 0)
    c = lax.broadcasted_iota(I32, (tm, tm), 1)
    before = jnp.where(c < r, 1.0, 0.0).astype(BF16)
    prior = jnp.dot(before, onehot.astype(BF16), preferred_element_type=F32) + carry_ref[...]
    r0 = jnp.sum(jnp.where(lane == e0, prior, 0.0), axis=1, keepdims=True)
    r1 = jnp.sum(jnp.where(lane == e1, prior, 0.0), axis=1, keepdims=True)
    carry_ref[...] = carry_ref[...] + jnp.sum(onehot, axis=0, keepdims=True)
    cnt_ref[...] = carry_ref[...]
    info = jnp.zeros_like(logits)
    for k, val in enumerate((e0.astype(F32), e1.astype(F32), r0, r1, w0, w1)):
        info = jnp.where(lane == k, val, info)
    info_ref[...] = info


def _router(x, g, w_router, b_router):
    t, d = x.shape
    tm = min(t, 256)
    return pl.pallas_call(
        _router_kernel,
        grid=(t // tm,),
        in_specs=[pl.BlockSpec((tm, d), lambda i: (i, 0)),
                  pl.BlockSpec((1, d), lambda i: (0, 0)),
                  pl.BlockSpec((d, LANES), lambda i: (0, 0)),
                  pl.BlockSpec((1, LANES), lambda i: (0, 0))],
        out_specs=[pl.BlockSpec((tm, d // 2), lambda i: (i, 0)),
                   pl.BlockSpec((tm, LANES), lambda i: (i, 0)),
                   pl.BlockSpec((1, LANES), lambda i: (0, 0))],
        out_shape=[jax.ShapeDtypeStruct((t, d // 2), U32),
                   jax.ShapeDtypeStruct((t, LANES), F32),
                   jax.ShapeDtypeStruct((1, LANES), F32)],
        scratch_shapes=[pltpu.VMEM((1, LANES), F32)],
        compiler_params=_cparams("arbitrary"),
    )(x, g.reshape(1, d), w_router, b_router)


def _plan_kernel(cnt_ref, info_ref, dest_ref, meta_ref):
    lane = lax.broadcasted_iota(I32, (8, LANES), 1)
    cnt = jnp.broadcast_to(cnt_ref[...], (8, LANES))
    nblk = jnp.floor((cnt + (MOE_BLOCK - 1)) * (1.0 / MOE_BLOCK))
    r = lax.broadcasted_iota(I32, (LANES, LANES), 0)
    c = lax.broadcasted_iota(I32, (LANES, LANES), 1)
    upto = jnp.where(r <= c, 1.0, 0.0).astype(BF16)
    end_blk = jnp.dot(nblk.astype(BF16), upto, preferred_element_type=F32)
    start_row = (end_blk - nblk) * MOE_BLOCK

    info = info_ref[...]
    tl = lax.broadcasted_iota(I32, info.shape, 1)
    e0, e1, r0, r1 = (info[:, k:k + 1] for k in range(4))
    start = start_row[:1, :]
    d0 = jnp.sum(jnp.where(tl == e0.astype(I32), start, 0.0), axis=1, keepdims=True) + r0
    d1 = jnp.sum(jnp.where(tl == e1.astype(I32), start, 0.0), axis=1, keepdims=True) + r1
    dest_ref[...] = jnp.where(tl == 0, d0, jnp.where(tl == 1, d1, 0.0)).astype(I32)

    @pl.when(pl.program_id(0) == 0)
    def _():
        nb = meta_ref.shape[0]
        blk = lax.broadcasted_iota(I32, (nb, LANES), 0).astype(F32)
        ml = lax.broadcasted_iota(I32, (nb, LANES), 1)
        done = jnp.where((ml < N_EXPERTS) & (end_blk[:1, :] <= blk), 1.0, 0.0)
        block_e = jnp.minimum(jnp.sum(done, axis=1, keepdims=True), N_EXPERTS - 1.0)
        n_used = jnp.sum(jnp.where(lane[:1, :] == N_EXPERTS - 1, end_blk[:1, :], 0.0), axis=1, keepdims=True)
        meta_ref[...] = jnp.where(ml == 0, block_e, jnp.where(ml == 1, n_used, 0.0)).astype(I32)


def _plan(counts, info, n_blocks):
    t = info.shape[0]
    tm = min(t, 1024)
    nb = -(-n_blocks // 8) * 8
    return pl.pallas_call(
        _plan_kernel,
        grid=(t // tm,),
        in_specs=[pl.BlockSpec((1, LANES), lambda i: (0, 0)), pl.BlockSpec((tm, LANES), lambda i: (i, 0))],
        out_specs=[pl.BlockSpec((tm, LANES), lambda i: (i, 0)), pl.BlockSpec((nb, LANES), lambda i: (0, 0))],
        out_shape=[jax.ShapeDtypeStruct((t, LANES), I32), jax.ShapeDtypeStruct((nb, LANES), I32)],
        compiler_params=_cparams("arbitrary"),
    )(counts, info)


def _row_copy(src, src_row, dst, dst_row, n, sem):
    return pltpu.make_async_copy(src.at[pl.ds(src_row, n)], dst.at[pl.ds(dst_row, n)], sem)


def _dispatch_kernel(dest_ref, hp_ref, xs_in_ref, xs_ref, sem, *, tokens_per_step):
    del xs_in_ref
    base = pl.program_id(0) * tokens_per_step

    def issue(j, carry):
        tok = base + j
        _row_copy(hp_ref, j, xs_ref, dest_ref[2 * tok], 1, sem).start()
        _row_copy(hp_ref, j, xs_ref, dest_ref[2 * tok + 1], 1, sem).start()
        return carry

    lax.fori_loop(0, tokens_per_step, issue, 0)
    _row_copy(hp_ref, 0, xs_ref, 0, tokens_per_step, sem).wait()
    _row_copy(hp_ref, 0, xs_ref, 0, tokens_per_step, sem).wait()


def _dispatch(dest, hp, n_rows):
    t, wd = hp.shape
    tc = min(t, 512)
    xs0 = jnp.zeros((n_rows, wd), U32)
    return pl.pallas_call(
        functools.partial(_dispatch_kernel, tokens_per_step=tc),
        grid_spec=pltpu.PrefetchScalarGridSpec(
            num_scalar_prefetch=1,
            grid=(t // tc,),
            in_specs=[pl.BlockSpec((tc, wd), lambda i, dr: (i, 0)), pl.BlockSpec(memory_space=pl.ANY)],
            out_specs=pl.BlockSpec(memory_space=pl.ANY),
            scratch_shapes=[pltpu.SemaphoreType.DMA(())],
        ),
        out_shape=jax.ShapeDtypeStruct((n_rows, wd), U32),
        input_output_aliases={2: 0},
        compiler_params=_cparams("arbitrary"),
    )(dest, hp, xs0)


def _expert_kernel(be_ref, nu_ref, xs_ref, wg_ref, wu_ref, wd_ref, o_ref):
    del be_ref

    @pl.when(pl.program_id(0) < nu_ref[0])
    def _():
        u = xs_ref[...]
        half = u.shape[1]
        lo = pltpu.bitcast(u << 16, F32).astype(BF16)
        hi = pltpu.bitcast(u & jnp.uint32(0xFFFF0000), F32).astype(BF16)

        def proj(w_ref):
            return (jnp.dot(lo, w_ref[0, :half, :], preferred_element_type=F32)
                    + jnp.dot(hi, w_ref[0, half:, :], preferred_element_type=F32))

        gate = proj(wg_ref)
        hid = gate * _sigmoid(gate) * proj(wu_ref)
        o_ref[...] = jnp.dot(hid.astype(BF16), wd_ref[0], preferred_element_type=F32)

    @pl.when(pl.program_id(0) >= nu_ref[0])
    def _():
        o_ref[...] = jnp.zeros_like(o_ref)


def _experts(block_e, n_used, xs, w_e_gate, w_e_up, w_e_down):
    n_rows, half = xs.shape
    d = 2 * half
    hid = w_e_gate.shape[2]
    n_blocks = n_rows // MOE_BLOCK
    blk = lambda i, be, nu: jnp.minimum(i, nu[0] - 1)
    return pl.pallas_call(
        _expert_kernel,
        grid_spec=pltpu.PrefetchScalarGridSpec(
            num_scalar_prefetch=2,
            grid=(n_blocks,),
            in_specs=[pl.BlockSpec((MOE_BLOCK, half), lambda i, be, nu: (blk(i, be, nu), 0)),
                      pl.BlockSpec((1, d, hid), lambda i, be, nu: (be[blk(i, be, nu)], 0, 0)),
                      pl.BlockSpec((1, d, hid), lambda i, be, nu: (be[blk(i, be, nu)], 0, 0)),
                      pl.BlockSpec((1, hid, d), lambda i, be, nu: (be[blk(i, be, nu)], 0, 0))],
            out_specs=pl.BlockSpec((MOE_BLOCK, d), lambda i, be, nu: (i, 0)),
        ),
        out_shape=jax.ShapeDtypeStruct((n_rows, d), F32),
        compiler_params=_cparams("arbitrary"),
    )(block_e, n_used, xs, w_e_gate, w_e_up, w_e_down)


def _combine_kernel(dest_ref, x_ref, info_ref, g_ref, y_ref, *rest, tokens_per_step, final):
    if final:
        o_ref, buf, sem = rest
    else:
        x2_ref, o_ref, buf, sem = rest
    step = pl.program_id(0)

    def gather(tile, slot):
        def issue(j, carry):
            tok = tile * tokens_per_step + j
            _row_copy(y_ref, dest_ref[2 * tok], buf.at[slot, 0], j, 1, sem.at[slot]).start()
            _row_copy(y_ref, dest_ref[2 * tok + 1], buf.at[slot, 1], j, 1, sem.at[slot]).start()
            return carry

        lax.fori_loop(0, tokens_per_step, issue, 0)

    @pl.when(step == 0)
    def _():
        gather(0, 0)

    @pl.when(step + 1 < pl.num_programs(0))
    def _():
        gather(step + 1, (step + 1) % 2)

    slot = step % 2
    _row_copy(y_ref, 0, buf.at[slot, 0], 0, tokens_per_step, sem.at[slot]).wait()
    _row_copy(y_ref, 0, buf.at[slot, 1], 0, tokens_per_step, sem.at[slot]).wait()
    info = info_ref[...]
    x2 = x_ref[...] + (buf[slot, 0] * info[:, 4:5] + buf[slot, 1] * info[:, 5:6])
    if not final:
        x2_ref[...] = x2
    o_ref[...] = _rms(x2, g_ref[...]).astype(o_ref.dtype)


def _combine(dest, x, info, g_next, y_rows, final):
    t, d = x.shape
    tc = min(t, 128)
    tile = pl.BlockSpec((tc, d), lambda i, dr: (i, 0))
    out_specs = [tile] if final else [tile, tile]
    out_shape = ([jax.ShapeDtypeStruct((t, d), F32)] if final
                 else [jax.ShapeDtypeStruct((t, d), F32), jax.ShapeDtypeStruct((t, d), BF16)])
    return pl.pallas_call(
        functools.partial(_combine_kernel, tokens_per_step=tc, final=final),
        grid_spec=pltpu.PrefetchScalarGridSpec(
            num_scalar_prefetch=1,
            grid=(t // tc,),
            in_specs=[tile,
                      pl.BlockSpec((tc, LANES), lambda i, dr: (i, 0)),
                      pl.BlockSpec((1, d), lambda i, dr: (0, 0)),
                      pl.BlockSpec(memory_space=pl.ANY)],
            out_specs=out_specs,
            scratch_shapes=[pltpu.VMEM((2, 2, tc, d), F32), pltpu.SemaphoreType.DMA((2,))],
        ),
        out_shape=out_shape,
        compiler_params=_cparams("arbitrary"),
    )(dest, x, info, g_next.reshape(1, d), y_rows)


def _pad_cols(w, width):
    return jnp.pad(w, ((0, 0), (0, width - w.shape[1])))


def _split_w_in(w_in, d):
    dsa_heads = d // (4 * HEAD_DIM)
    hpg = d // 1024
    dil_heads = len(DIL_PATTERNS) * hpg
    a_w, b_w = dsa_heads * HEAD_DIM, dil_heads * HEAD_DIM
    pool_w = d - a_w - b_w
    sizes = (a_w, HEAD_DIM, HEAD_DIM, IDX_HEADS * IDX_DIM, IDX_DIM, IDX_HEADS, b_w, b_w, b_w, pool_w)
    offs = [0]
    for s in sizes:
        offs.append(offs[-1] + s)
    qa, ka, va, qi, ki, wi, qb, kb, vb, c = (w_in[:, offs[n]:offs[n + 1]] for n in range(len(sizes)))
    zero64 = jnp.zeros_like(ki)
    mxu_pad = lambda w: _pad_cols(w, -(-w.shape[1] // MXU_WIDTH) * MXU_WIDTH)
    w_rope = mxu_pad(jnp.concatenate([qa, ka, qb, kb], axis=1))
    w_idx = mxu_pad(jnp.concatenate([qi, ki, zero64, zero64, ki], axis=1))
    w_val = mxu_pad(jnp.concatenate([va, vb], axis=1))
    w_c = mxu_pad(jnp.concatenate([c, wi], axis=1))
    lay = dict(dsa_heads=dsa_heads, dil_heads_per_group=hpg,
               ka=a_w, qb=a_w + HEAD_DIM, kb=a_w + HEAD_DIM + b_w,
               ki_lo=IDX_HEADS * IDX_DIM, ki_hi=IDX_HEADS * IDX_DIM + LANES,
               va=0, vb=HEAD_DIM, wi=pool_w)
    return tuple(w.astype(BF16) for w in (w_rope, w_idx, w_val, w_c)), lay


def _col_tile(n, cap=512):
    best = LANES
    for m in range(1, cap // LANES + 1):
        if n % (m * LANES) == 0:
            best = m * LANES
    return best


def kernel(x, positions, norm1_g, w_in, w_gate, b_gate, w_br_a, w_br_b, w_br_c, w_pool, pool_scale,
           w_out, norm2_g, w_rg, b_rg, w_re, b_re, w_e_gate, w_e_up, w_e_down, final_g):
    bsz, seq, d = x.shape
    depth = w_in.shape[0]
    t = bsz * seq
    n_blocks = -(-(2 * t) // MOE_BLOCK) + N_EXPERTS
    n_rows = n_blocks * MOE_BLOCK

    c128, s128, c64, s64 = _rope_tables(positions)
    xt = x.reshape(t, d)
    h = _rms_norm(xt, norm1_g[0], BF16)
    out = None
    for l in range(depth):
        (w_rope, w_idx, w_val, w_c), lay = _split_w_in(w_in[l], d)
        zr = _project(h, w_rope, BF16, _col_tile(w_rope.shape[1]), rope=(HEAD_DIM, c128, s128))
        zi = _project(h, w_idx, BF16, _col_tile(w_idx.shape[1]), rope=(IDX_DIM, c64, s64))
        zv = _project(h, w_val, BF16, _col_tile(w_val.shape[1]))
        zc = _project(h, w_c, F32, _col_tile(w_c.shape[1]))

        a_out = _dsa(zr, zi, zv, zc, bsz, seq, lay)
        b_out = _dilated(zr, zv, bsz, seq, lay)
        c_out = _pool(zc, w_pool[l].astype(BF16), pool_scale[l], bsz, seq)

        merged = _merge(h, w_gate[l].astype(BF16), b_gate[l], a_out, b_out, c_out,
                        w_br_a[l].astype(BF16), w_br_b[l].astype(BF16), w_br_c[l].astype(BF16))
        x1 = _outproj(merged, w_out[l].astype(BF16), xt)

        w_router = _pad_cols(jnp.concatenate([w_re[l], w_rg[l]], axis=1), LANES).astype(BF16)
        b_router = _pad_cols(jnp.concatenate([b_re[l], b_rg[l]])[None, :], LANES)
        hp, info, counts = _router(x1, norm2_g[l], w_router, b_router)
        dest2, meta = _plan(counts, info, n_blocks)
        dest = dest2[:, :2].reshape(2 * t)
        block_e = meta[:n_blocks, 0]
        n_used = meta[:1, 1]
        xs = _dispatch(dest, hp, n_rows)
        y_rows = _experts(block_e, n_used, xs, w_e_gate[l].astype(BF16), w_e_up[l].astype(BF16),
                          w_e_down[l].astype(BF16))
        final = l == depth - 1
        g_next = final_g if final else norm1_g[l + 1]
        res = _combine(dest, x1, info, g_next, y_rows, final)
        if final:
            out = res[0]
        else:
            xt, h = res
    return out.reshape(bsz, seq, d)
```

```python
import functools

import jax
import jax.numpy as jnp
from jax import lax
from jax.experimental import pallas as pl
from jax.experimental.pallas import tpu as pltpu

F32 = jnp.float32
BF16 = jnp.bfloat16
I32 = jnp.int32
U32 = jnp.uint32

LANES = 128
MXU_WIDTH = 256
VMEM_LIMIT = 56 * 1024 * 1024

HEAD_DIM = 128
ROPE_THETA = 500000.0
ROPE_FRACTION = 4
NORM_EPS = 1e-6
IDX_HEADS = 16
IDX_DIM = 64
DSA_TOPK = 256
Q_BLOCK = 128
DIL_PATTERNS = ((128, 1), (512, 4), (2048, 16))
POOL_WINDOWS = (2, 4, 8, 16)
N_EXPERT_GROUPS = 4
EXPERTS_PER_GROUP = 8
N_EXPERTS = N_EXPERT_GROUPS * EXPERTS_PER_GROUP
MOE_BLOCK = 256
NEG_INF = float("-inf")


def _cparams(*sem):
    return pltpu.CompilerParams(dimension_semantics=sem, vmem_limit_bytes=VMEM_LIMIT)


def _rope_table_kernel(pos_ref, f128_ref, s128_ref, f64_ref, s64_ref, c128, sn128, c64, sn64):
    p = pos_ref[...].astype(F32)
    a = p * f128_ref[...]
    c128[...] = jnp.cos(a)
    sn128[...] = jnp.sin(a) * s128_ref[...]
    a = p * f64_ref[...]
    c64[...] = jnp.cos(a)
    sn64[...] = jnp.sin(a) * s64_ref[...]


def _rope_lane_patterns():
    lane = jnp.arange(LANES)

    def pattern(width):
        rd = width // ROPE_FRACTION
        half = rd // 2
        inv = ROPE_THETA ** (-jnp.arange(half, dtype=F32) * (2.0 / rd))
        l = lane % width
        freq = jnp.where(l < rd, inv[l % half], 0.0).astype(F32)
        sign = jnp.where(l < half, -1.0, jnp.where(l < rd, 1.0, 0.0)).astype(F32)
        return freq[None, :], sign[None, :]

    return pattern(HEAD_DIM) + pattern(IDX_DIM)


def _rope_tables(positions):
    t = positions.size
    tm = min(t, 1024)
    f128, s128, f64, s64 = _rope_lane_patterns()
    vec = pl.BlockSpec((1, LANES), lambda i: (0, 0))
    tab = pl.BlockSpec((tm, LANES), lambda i: (i, 0))
    return pl.pallas_call(
        _rope_table_kernel,
        grid=(t // tm,),
        in_specs=[pl.BlockSpec((tm, 1), lambda i: (i, 0)), vec, vec, vec, vec],
        out_specs=[tab] * 4,
        out_shape=[jax.ShapeDtypeStruct((t, LANES), F32)] * 4,
        compiler_params=_cparams("arbitrary"),
    )(positions.reshape(t, 1), f128, s128, f64, s64)


def _rms(x, g):
    ms = jnp.mean(x * x, axis=-1, keepdims=True)
    return x * lax.rsqrt(ms + NORM_EPS) * g


def _norm_kernel(x_ref, g_ref, o_ref):
    o_ref[...] = _rms(x_ref[...], g_ref[...]).astype(o_ref.dtype)


def _rms_norm(x, g, out_dtype):
    t, d = x.shape
    tm = min(t, 512)
    return pl.pallas_call(
        _norm_kernel,
        grid=(t // tm,),
        in_specs=[pl.BlockSpec((tm, d), lambda i: (i, 0)), pl.BlockSpec((1, d), lambda i: (0, 0))],
        out_specs=pl.BlockSpec((tm, d), lambda i: (i, 0)),
        out_shape=jax.ShapeDtypeStruct((t, d), out_dtype),
        compiler_params=_cparams("arbitrary"),
    )(x, g.reshape(1, d))


PROJ_SUB_ROWS = 512


def _proj_kernel(x_ref, w_ref, *rest, rope_width):
    o_ref = rest[-1]
    sub = min(PROJ_SUB_ROWS, x_ref.shape[0])
    for r in range(x_ref.shape[0] // sub):
        rows = pl.ds(r * sub, sub)
        acc = jnp.dot(x_ref[rows, :], w_ref[...], preferred_element_type=F32)
        if rope_width is None:
            o_ref[rows, :] = acc.astype(o_ref.dtype)
            continue
        c_ref, s_ref = rest[:2]
        half = rope_width // ROPE_FRACTION // 2
        c = c_ref[rows, :]
        s = s_ref[rows, :]
        lane = lax.broadcasted_iota(I32, c.shape, 1)
        first_half = (lane % rope_width) < half
        for g in range(acc.shape[1] // LANES):
            z = acc[:, g * LANES:(g + 1) * LANES]
            partner = jnp.where(first_half, pltpu.roll(z, LANES - half, 1), pltpu.roll(z, half, 1))
            o_ref[rows, g * LANES:(g + 1) * LANES] = (z * c + partner * s).astype(o_ref.dtype)


def _project(h, w, out_dtype, tn, rope=None):
    t, k = h.shape
    n = w.shape[1]
    tm = min(t, 2048)
    in_specs = [pl.BlockSpec((tm, k), lambda i, j: (i, 0)), pl.BlockSpec((k, tn), lambda i, j: (0, j))]
    args = [h, w]
    if rope is not None:
        tab = pl.BlockSpec((tm, LANES), lambda i, j: (i, 0))
        in_specs += [tab, tab]
        args += [rope[1], rope[2]]
    return pl.pallas_call(
        functools.partial(_proj_kernel, rope_width=None if rope is None else rope[0]),
        grid=(t // tm, n // tn),
        in_specs=in_specs,
        out_specs=pl.BlockSpec((tm, tn), lambda i, j: (i, j)),
        out_shape=jax.ShapeDtypeStruct((t, n), out_dtype),
        compiler_params=_cparams("arbitrary", "arbitrary"),
    )(*args)


KEY_CHUNK = 256
SUBLANES = 8


def _sublane_allreduce(x, op):
    for shift in (4, 2, 1):
        x = op(x, pltpu.roll(x, shift, 0))
    return x


def _tree_reduce(x, op):
    while x.shape[0] > 1:
        half = x.shape[0] // 2
        x = op(x[:half], x[half:])
    return x[0]


def _vregs(x):
    return x.reshape(x.shape[0] // SUBLANES, SUBLANES, LANES)


def _dsa_kernel(qa_ref, ka_ref, va_ref, qi_ref, kil_ref, kih_ref, wi_ref, o_ref,
                key_ref, vat_ref, acc_ref, m_ref, l_ref, *, topk, n_heads):
    nq = o_ref.shape[0]
    ck = KEY_CHUNK
    i = pl.program_id(1)
    t0 = i * nq
    n_chunks = (t0 + nq + ck - 1) // ck
    nt = (((1,), (1,)), ((), ()))

    @pl.when(i == 0)
    def _():
        for c in range(vat_ref.shape[0]):
            vat_ref[c] = va_ref[c * ck:(c + 1) * ck, :].astype(F32).T.astype(BF16)

    def causal_mask(c):
        shape = (ck // SUBLANES, SUBLANES, nq)
        spos = (c * ck + lax.broadcasted_iota(I32, shape, 0) * SUBLANES
                + lax.broadcasted_iota(I32, shape, 1))
        tpos = t0 + lax.broadcasted_iota(I32, shape, 2)
        return spos <= tpos

    wt = (wi_ref[...] * ((IDX_DIM ** -0.5) * (IDX_HEADS ** -0.5))).T
    w8 = [jnp.broadcast_to(wt[h:h + 1, :], (SUBLANES, LANES)) for h in range(IDX_HEADS)]
    q_stack = [jnp.concatenate([qi_ref[:, (2 * m) * LANES:(2 * m + 1) * LANES],
                                qi_ref[:, (2 * m + 1) * LANES:(2 * m + 2) * LANES]], axis=0)
               for m in range(IDX_HEADS // 4)]

    def score_chunk(c, carry):
        r0 = pl.multiple_of(c * ck, ck)
        kil = kil_ref[pl.ds(r0, ck), :]
        kih = kih_ref[pl.ds(r0, ck), :]
        acc = jnp.zeros((ck // SUBLANES, SUBLANES, LANES), F32)
        for m in range(IDX_HEADS // 4):
            d_even = lax.dot_general(kil, q_stack[m], nt, preferred_element_type=F32)
            d_odd = lax.dot_general(kih, q_stack[m], nt, preferred_element_type=F32)
            for dots, h in ((d_even[:, :nq], 4 * m), (d_odd[:, :nq], 4 * m + 1),
                            (d_even[:, nq:], 4 * m + 2), (d_odd[:, nq:], 4 * m + 3)):
                acc = acc + jnp.maximum(_vregs(dots), 0.0) * w8[h][None]
        score = jnp.where(causal_mask(c), acc, NEG_INF).reshape(ck, nq)
        bits = pltpu.bitcast(score, I32)
        key_ref[c] = bits ^ ((bits >> 31) & jnp.int32(0x7FFFFFFF))
        return carry

    lax.fori_loop(0, n_chunks, score_chunk, 0)

    def count_ge(cand):
        def body(c, cnt):
            return cnt + _tree_reduce(jnp.where(_vregs(key_ref[c]) >= cand[None], 1.0, 0.0), jnp.add)
        cnt = lax.fori_loop(0, n_chunks, body, jnp.zeros((SUBLANES, LANES), F32))
        return _sublane_allreduce(cnt, jnp.add)

    int_min = jnp.int32(-2 ** 31)
    thr = jnp.where(count_ge(jnp.zeros((SUBLANES, LANES), I32)) >= topk, jnp.int32(0), int_min)

    def refine(it, thr):
        cand = thr | (jnp.int32(1) << (30 - it))
        return jnp.where(count_ge(cand) >= topk, cand, thr)

    thr = lax.fori_loop(0, 31, refine, thr)

    qt = jnp.concatenate([qa_ref[:, h * HEAD_DIM:(h + 1) * HEAD_DIM].astype(F32).T for h in range(n_heads)],
                         axis=1).astype(BF16)
    m_ref[...] = jnp.full(m_ref.shape, NEG_INF, F32)
    l_ref[...] = jnp.zeros(l_ref.shape, F32)
    acc_ref[...] = jnp.zeros(acc_ref.shape, F32)

    def attend(c, carry):
        r0 = pl.multiple_of(c * ck, ck)
        keep = (_vregs(key_ref[c]) >= thr[None]) & causal_mask(c)
        bias = jnp.where(keep, 0.0, NEG_INF)
        s_all = jnp.dot(ka_ref[pl.ds(r0, ck), :], qt, preferred_element_type=F32) * (HEAD_DIM ** -0.5)
        ps, alphas = [], []
        for h in range(n_heads):
            s = _vregs(s_all[:, h * nq:(h + 1) * nq]) + bias
            m_old = m_ref[h]
            m_new = jnp.maximum(m_old, _sublane_allreduce(_tree_reduce(s, jnp.maximum), jnp.maximum))
            m_safe = jnp.where(m_new == NEG_INF, 0.0, m_new)
            alphas.append(jnp.exp(m_old - m_safe))
            p = jnp.exp(s - m_safe[None])
            l_ref[h] = alphas[h] * l_ref[h] + _tree_reduce(p, jnp.add)
            ps.append(p.reshape(ck, nq).astype(BF16))
            m_ref[h] = m_new
        pv = jnp.dot(vat_ref[c], jnp.concatenate(ps, axis=1), preferred_element_type=F32)
        for h in range(n_heads):
            acc_ref[h] = (_vregs(acc_ref[h]) * alphas[h][None]).reshape(HEAD_DIM, nq) + pv[:, h * nq:(h + 1) * nq]
        return carry

    lax.fori_loop(0, n_chunks, attend, 0)
    for h in range(n_heads):
        l = _sublane_allreduce(l_ref[h], jnp.add)
        o = (_vregs(acc_ref[h]) / l[None]).reshape(HEAD_DIM, nq)
        o_ref[:, h * HEAD_DIM:(h + 1) * HEAD_DIM] = o.T.astype(o_ref.dtype)


def _dsa(zr, zi, zv, zc, bsz, seq, lay):
    t = bsz * seq
    nqb = seq // Q_BLOCK
    n_heads = lay["dsa_heads"]
    a_w = n_heads * HEAD_DIM
    qw = IDX_HEADS * IDX_DIM
    nck = seq // KEY_CHUNK
    row = lambda b, i: b * nqb + i
    in_specs = [
        pl.BlockSpec((Q_BLOCK, a_w), lambda b, i: (row(b, i), 0)),
        pl.BlockSpec((seq, HEAD_DIM), lambda b, i: (b, lay["ka"] // HEAD_DIM)),
        pl.BlockSpec((seq, HEAD_DIM), lambda b, i: (b, lay["va"] // HEAD_DIM)),
        pl.BlockSpec((Q_BLOCK, qw), lambda b, i: (row(b, i), 0)),
        pl.BlockSpec((seq, LANES), lambda b, i: (b, lay["ki_lo"] // LANES)),
        pl.BlockSpec((seq, LANES), lambda b, i: (b, lay["ki_hi"] // LANES)),
        pl.BlockSpec((Q_BLOCK, LANES), lambda b, i: (row(b, i), lay["wi"] // LANES)),
    ]
    return pl.pallas_call(
        functools.partial(_dsa_kernel, topk=min(DSA_TOPK, seq // 4), n_heads=n_heads),
        grid=(bsz, nqb),
        in_specs=in_specs,
        out_specs=pl.BlockSpec((Q_BLOCK, a_w), lambda b, i: (row(b, i), 0)),
        out_shape=jax.ShapeDtypeStruct((t, a_w), BF16),
        scratch_shapes=[pltpu.VMEM((nck, KEY_CHUNK, Q_BLOCK), I32),
                        pltpu.VMEM((nck, HEAD_DIM, KEY_CHUNK), BF16),
                        pltpu.VMEM((n_heads, HEAD_DIM, Q_BLOCK), F32),
                        pltpu.VMEM((n_heads, SUBLANES, Q_BLOCK), F32),
                        pltpu.VMEM((n_heads, SUBLANES, Q_BLOCK), F32)],
        compiler_params=_cparams("arbitrary", "arbitrary"),
    )(zr, zr, zv, zi, zi, zi, zc)


MIX_ROWS = 256


def _band_attention(q, k_blocks, v_blocks):
    bs = q.shape[0]
    nt = (((1,), (1,)), ((), ()))
    k = k_blocks[0] if len(k_blocks) == 1 else jnp.concatenate(k_blocks, axis=0)
    v = v_blocks[0] if len(v_blocks) == 1 else jnp.concatenate(v_blocks, axis=0)
    s = lax.dot_general(q, k, nt, preferred_element_type=F32) * (HEAD_DIM ** -0.5)
    a = lax.broadcasted_iota(I32, s.shape, 0)
    c = lax.broadcasted_iota(I32, s.shape, 1)
    if len(k_blocks) == 1:
        valid = c <= a
    else:
        valid = (a + bs - c).astype(U32) <= jnp.uint32(bs)
    s = jnp.where(valid, s, NEG_INF)
    m = jnp.max(s, axis=1, keepdims=True)
    p = jnp.exp(s - m)
    l = jnp.sum(p, axis=1, keepdims=True)
    o = jnp.dot(p.astype(BF16), v, preferred_element_type=F32) / l
    return o, m + jnp.log(l)


def _dil_kernel(*refs, seq):
    ng = len(DIL_PATTERNS)
    q_refs, k_refs, v_refs = refs[:ng], refs[ng:2 * ng], refs[2 * ng:3 * ng]
    o_ref, qf, kf, vf, of, lf = refs[3 * ng:]
    for g, (window, dil) in enumerate(DIL_PATTERNS):
        ls = seq // dil
        bs = min(ls, Q_BLOCK)
        assert ls % bs == 0 and window // dil == Q_BLOCK
        if dil > 1:
            qf[...] = q_refs[g][...].astype(F32)
            kf[...] = k_refs[g][...].astype(F32)
            vf[...] = v_refs[g][...].astype(F32)
        for x in range(dil):
            def rows(n):
                return pl.ds(x + dil * bs * n, bs, stride=dil) if dil > 1 else pl.ds(bs * n, bs)

            def load(src_bf16, src_f32, n):
                return src_bf16[rows(n), :] if dil == 1 else src_f32[rows(n), :].astype(BF16)

            k_prev = v_prev = None
            for n in range(ls // bs):
                q = load(q_refs[g], qf, n)
                k_cur = load(k_refs[g], kf, n)
                v_cur = load(v_refs[g], vf, n)
                if n == 0:
                    o, lse = _band_attention(q, [k_cur], [v_cur])
                else:
                    o, lse = _band_attention(q, [k_prev, k_cur], [v_prev, v_cur])
                of[g, rows(n), :] = o
                lf[g, rows(n), :] = jnp.broadcast_to(lse, o.shape)
                k_prev, v_prev = k_cur, v_cur
    mr = min(MIX_ROWS, seq)
    for r in range(seq // mr):
        sl = pl.ds(r * mr, mr)
        lses = [lf[g, sl, :] for g in range(ng)]
        mx = functools.reduce(jnp.maximum, lses)
        es = [jnp.exp(x - mx) for x in lses]
        tot = functools.reduce(lambda u, w: u + w, es)
        acc = functools.reduce(lambda u, w: u + w, [(e / tot) * of[g, sl, :] for g, e in enumerate(es)])
        o_ref[sl, :] = acc.astype(o_ref.dtype)


def _dilated(zr, zv, bsz, seq, lay):
    t = bsz * seq
    hpg = lay["dil_heads_per_group"]
    ng = len(DIL_PATTERNS)

    def head_spec(base, g):
        return pl.BlockSpec((seq, HEAD_DIM), lambda b, s: (b, base // HEAD_DIM + g * hpg + s))

    in_specs = ([head_spec(lay["qb"], g) for g in range(ng)]
                + [head_spec(lay["kb"], g) for g in range(ng)]
                + [head_spec(lay["vb"], g) for g in range(ng)])
    return pl.pallas_call(
        functools.partial(_dil_kernel, seq=seq),
        grid=(bsz, hpg),
        in_specs=in_specs,
        out_specs=pl.BlockSpec((seq, HEAD_DIM), lambda b, s: (b, s)),
        out_shape=jax.ShapeDtypeStruct((t, hpg * HEAD_DIM), BF16),
        scratch_shapes=[pltpu.VMEM((seq, HEAD_DIM), F32)] * 3
                       + [pltpu.VMEM((ng, seq, HEAD_DIM), F32)] * 2,
        compiler_params=_cparams("arbitrary", "arbitrary"),
    )(*([zr] * (2 * ng) + [zv] * ng))


def _pool_kernel(c_ref, wp_ref, ps_ref, o_ref):
    x = c_ref[...]
    row = lax.broadcasted_iota(I32, x.shape, 0)

    def run(window):
        acc = x
        k = 1
        while k < window:
            acc = acc + jnp.where(row >= k, pltpu.roll(acc, k, 0), 0.0)
            k *= 2
        denom = jnp.minimum(row[:, :1] + 1, window).astype(F32)
        y = (acc / denom - x).astype(BF16)
        o = jnp.dot(y, wp_ref[0], preferred_element_type=F32) * ps_ref[...]
        o_ref[...] = o.astype(o_ref.dtype)

    for gi, window in enumerate(POOL_WINDOWS):
        pl.when(pl.program_id(1) == gi)(functools.partial(run, window))


def _pool(zc, w_pool, pool_scale, bsz, seq):
    t = bsz * seq
    ng, gw = w_pool.shape[0], w_pool.shape[1]
    return pl.pallas_call(
        _pool_kernel,
        grid=(bsz, ng),
        in_specs=[pl.BlockSpec((seq, gw), lambda b, g: (b, g)),
                  pl.BlockSpec((1, gw, gw), lambda b, g: (g, 0, 0)),
                  pl.BlockSpec((1, gw), lambda b, g: (0, g))],
        out_specs=pl.BlockSpec((seq, gw), lambda b, g: (b, g)),
        out_shape=jax.ShapeDtypeStruct((t, ng * gw), BF16),
        compiler_params=_cparams("arbitrary", "arbitrary"),
    )(zc, w_pool, pool_scale.reshape(1, ng * gw))


def _sigmoid(x):
    return 1.0 / (1.0 + jnp.exp(-x))


def _merge_kernel(h_ref, wg0, wg1, wg2, bg0, bg1, bg2, a_ref, b_ref, c_ref, wa, wb, wc, o_ref):
    h = h_ref[...]
    acc = None
    for wg, bg, br, wbr in ((wg0, bg0, a_ref, wa), (wg1, bg1, b_ref, wb), (wg2, bg2, c_ref, wc)):
        gate = _sigmoid(jnp.dot(h, wg[...], preferred_element_type=F32) + bg[...])
        term = gate * jnp.dot(br[...], wbr[...], preferred_element_type=F32)
        acc = term if acc is None else acc + term
    o_ref[...] = acc.astype(o_ref.dtype)


def _merge(h, w_gate, b_gate, a_out, b_out, c_out, w_a, w_b, w_c):
    t, d = h.shape
    tm = min(t, 1024)
    tn = 256
    nd = d // tn
    rows = lambda width: pl.BlockSpec((tm, width), lambda i, j: (i, 0))
    cols = lambda k, off: pl.BlockSpec((k, tn), lambda i, j: (0, off * nd + j))
    in_specs = ([rows(d)] + [cols(d, n) for n in range(3)] + [cols(1, n) for n in range(3)]
                + [rows(a_out.shape[1]), rows(b_out.shape[1]), rows(c_out.shape[1])]
                + [cols(a_out.shape[1], 0), cols(b_out.shape[1], 0), cols(c_out.shape[1], 0)])
    bg = b_gate.reshape(1, 3 * d)
    return pl.pallas_call(
        _merge_kernel,
        grid=(t // tm, nd),
        in_specs=in_specs,
        out_specs=pl.BlockSpec((tm, tn), lambda i, j: (i, j)),
        out_shape=jax.ShapeDtypeStruct((t, d), BF16),
        compiler_params=_cparams("arbitrary", "arbitrary"),
    )(h, w_gate, w_gate, w_gate, bg, bg, bg, a_out, b_out, c_out, w_a, w_b, w_c)


def _outproj_kernel(m_ref, w_ref, x_ref, o_ref):
    o_ref[...] = x_ref[...] + jnp.dot(m_ref[...], w_ref[...], preferred_element_type=F32)


def _outproj(merged, w_out, x):
    t, d = x.shape
    tm = min(t, 1024)
    tn = 512
    return pl.pallas_call(
        _outproj_kernel,
        grid=(t // tm, d // tn),
        in_specs=[pl.BlockSpec((tm, d), lambda i, j: (i, 0)),
                  pl.BlockSpec((d, tn), lambda i, j: (0, j)),
                  pl.BlockSpec((tm, tn), lambda i, j: (i, j))],
        out_specs=pl.BlockSpec((tm, tn), lambda i, j: (i, j)),
        out_shape=jax.ShapeDtypeStruct((t, d), F32),
        compiler_params=_cparams("arbitrary", "arbitrary"),
    )(merged, w_out, x)


GROUP_LANE0 = N_EXPERTS


def _lane_min_index(mask, lane):
    return jnp.min(jnp.where(mask, lane, LANES), axis=1, keepdims=True)


def _router_kernel(x_ref, g_ref, wr_ref, br_ref, hp_ref, info_ref, cnt_ref, carry_ref):
    @pl.when(pl.program_id(0) == 0)
    def _():
        carry_ref[...] = jnp.zeros_like(carry_ref)

    hb = _rms(x_ref[...], g_ref[...]).astype(BF16)
    tm, d = hb.shape
    lo = pltpu.bitcast(hb[:, :d // 2].astype(F32), U32)
    hi = pltpu.bitcast(hb[:, d // 2:].astype(F32), U32)
    hp_ref[...] = hi | (lo >> 16)

    logits = jnp.dot(hb, wr_ref[...], preferred_element_type=F32) + br_ref[...]
    lane = lax.broadcasted_iota(I32, logits.shape, 1)
    is_group = (lane >= GROUP_LANE0) & (lane < GROUP_LANE0 + N_EXPERT_GROUPS)
    lg = jnp.where(is_group, logits, NEG_INF)
    gmax = jnp.max(lg, axis=1, keepdims=True)
    gsel = _lane_min_index(lg == gmax, lane) - GROUP_LANE0
    pg_sel = 1.0 / jnp.sum(jnp.where(is_group, jnp.exp(lg - gmax), 0.0), axis=1, keepdims=True)
    in_group = (lane < N_EXPERTS) & ((lane // EXPERTS_PER_GROUP) == gsel)
    le = jnp.where(in_group, logits, NEG_INF)
    ee = jnp.where(in_group, jnp.exp(le - jnp.max(le, axis=1, keepdims=True)), 0.0)
    pe = jnp.where(in_group, ee / jnp.sum(ee, axis=1, keepdims=True), -1.0)
    v0 = jnp.max(pe, axis=1, keepdims=True)
    e0 = _lane_min_index(pe == v0, lane)
    pe1 = jnp.where(lane == e0, -1.0, pe)
    v1 = jnp.max(pe1, axis=1, keepdims=True)
    e1 = _lane_min_index(pe1 == v1, lane)
    w0 = pg_sel * v0 / (v0 + v1)
    w1 = pg_sel * v1 / (v0 + v1)
    onehot = jnp.where((lane == e0) | (lane == e1), 1.0, 0.0)
    r = lax.broadcasted_iota(I32, (tm, tm), 0)
    c = lax.broadcasted_iota(I32, (tm, tm), 1)
    before = jnp.where(c < r, 1.0, 0.0).astype(BF16)
    prior = jnp.dot(before, onehot.astype(BF16), preferred_element_type=F32) + carry_ref[...]
    r0 = jnp.sum(jnp.where(lane == e0, prior, 0.0), axis=1, keepdims=True)
    r1 = jnp.sum(jnp.where(lane == e1, prior, 0.0), axis=1, keepdims=True)
    carry_ref[...] = carry_ref[...] + jnp.sum(onehot, axis=0, keepdims=True)
    cnt_ref[...] = carry_ref[...]
    info = jnp.zeros_like(logits)
    for k, val in enumerate((e0.astype(F32), e1.astype(F32), r0, r1, w0, w1)):
        info = jnp.where(lane == k, val, info)
    info_ref[...] = info


def _router(x, g, w_router, b_router):
    t, d = x.shape
    tm = min(t, 256)
    return pl.pallas_call(
        _router_kernel,
        grid=(t // tm,),
        in_specs=[pl.BlockSpec((tm, d), lambda i: (i, 0)),
                  pl.BlockSpec((1, d), lambda i: (0, 0)),
                  pl.BlockSpec((d, LANES), lambda i: (0, 0)),
                  pl.BlockSpec((1, LANES), lambda i: (0, 0))],
        out_specs=[pl.BlockSpec((tm, d // 2), lambda i: (i, 0)),
                   pl.BlockSpec((tm, LANES), lambda i: (i, 0)),
                   pl.BlockSpec((1, LANES), lambda i: (0, 0))],
        out_shape=[jax.ShapeDtypeStruct((t, d // 2), U32),
                   jax.ShapeDtypeStruct((t, LANES), F32),
                   jax.ShapeDtypeStruct((1, LANES), F32)],
        scratch_shapes=[pltpu.VMEM((1, LANES), F32)],
        compiler_params=_cparams("arbitrary"),
    )(x, g.reshape(1, d), w_router, b_router)


def _plan_kernel(cnt_ref, info_ref, dest_ref, meta_ref):
    lane = lax.broadcasted_iota(I32, (8, LANES), 1)
    cnt = jnp.broadcast_to(cnt_ref[...], (8, LANES))
    nblk = jnp.floor((cnt + (MOE_BLOCK - 1)) * (1.0 / MOE_BLOCK))
    r = lax.broadcasted_iota(I32, (LANES, LANES), 0)
    c = lax.broadcasted_iota(I32, (LANES, LANES), 1)
    upto = jnp.where(r <= c, 1.0, 0.0).astype(BF16)
    end_blk = jnp.dot(nblk.astype(BF16), upto, preferred_element_type=F32)
    start_row = (end_blk - nblk) * MOE_BLOCK

    info = info_ref[...]
    tl = lax.broadcasted_iota(I32, info.shape, 1)
    e0, e1, r0, r1 = (info[:, k:k + 1] for k in range(4))
    start = start_row[:1, :]
    d0 = jnp.sum(jnp.where(tl == e0.astype(I32), start, 0.0), axis=1, keepdims=True) + r0
    d1 = jnp.sum(jnp.where(tl == e1.astype(I32), start, 0.0), axis=1, keepdims=True) + r1
    dest_ref[...] = jnp.where(tl == 0, d0, jnp.where(tl == 1, d1, 0.0)).astype(I32)

    @pl.when(pl.program_id(0) == 0)
    def _():
        nb = meta_ref.shape[0]
        blk = lax.broadcasted_iota(I32, (nb, LANES), 0).astype(F32)
        ml = lax.broadcasted_iota(I32, (nb, LANES), 1)
        done = jnp.where((ml < N_EXPERTS) & (end_blk[:1, :] <= blk), 1.0, 0.0)
        block_e = jnp.minimum(jnp.sum(done, axis=1, keepdims=True), N_EXPERTS - 1.0)
        n_used = jnp.sum(jnp.where(lane[:1, :] == N_EXPERTS - 1, end_blk[:1, :], 0.0), axis=1, keepdims=True)
        meta_ref[...] = jnp.where(ml == 0, block_e, jnp.where(ml == 1, n_used, 0.0)).astype(I32)


def _plan(counts, info, n_blocks):
    t = info.shape[0]
    tm = min(t, 1024)
    nb = -(-n_blocks // 8) * 8
    return pl.pallas_call(
        _plan_kernel,
        grid=(t // tm,),
        in_specs=[pl.BlockSpec((1, LANES), lambda i: (0, 0)), pl.BlockSpec((tm, LANES), lambda i: (i, 0))],
        out_specs=[pl.BlockSpec((tm, LANES), lambda i: (i, 0)), pl.BlockSpec((nb, LANES), lambda i: (0, 0))],
        out_shape=[jax.ShapeDtypeStruct((t, LANES), I32), jax.ShapeDtypeStruct((nb, LANES), I32)],
        compiler_params=_cparams("arbitrary"),
    )(counts, info)


def _row_copy(src, src_row, dst, dst_row, n, sem):
    return pltpu.make_async_copy(src.at[pl.ds(src_row, n)], dst.at[pl.ds(dst_row, n)], sem)


def _dispatch_kernel(dest_ref, hp_ref, xs_in_ref, xs_ref, sem, *, tokens_per_step):
    del xs_in_ref
    base = pl.program_id(0) * tokens_per_step

    def issue(j, carry):
        tok = base + j
        _row_copy(hp_ref, j, xs_ref, dest_ref[2 * tok], 1, sem).start()
        _row_copy(hp_ref, j, xs_ref, dest_ref[2 * tok + 1], 1, sem).start()
        return carry

    lax.fori_loop(0, tokens_per_step, issue, 0)
    _row_copy(hp_ref, 0, xs_ref, 0, tokens_per_step, sem).wait()
    _row_copy(hp_ref, 0, xs_ref, 0, tokens_per_step, sem).wait()


def _dispatch(dest, hp, n_rows):
    t, wd = hp.shape
    tc = min(t, 512)
    xs0 = jnp.zeros((n_rows, wd), U32)
    return pl.pallas_call(
        functools.partial(_dispatch_kernel, tokens_per_step=tc),
        grid_spec=pltpu.PrefetchScalarGridSpec(
            num_scalar_prefetch=1,
            grid=(t // tc,),
            in_specs=[pl.BlockSpec((tc, wd), lambda i, dr: (i, 0)), pl.BlockSpec(memory_space=pl.ANY)],
            out_specs=pl.BlockSpec(memory_space=pl.ANY),
            scratch_shapes=[pltpu.SemaphoreType.DMA(())],
        ),
        out_shape=jax.ShapeDtypeStruct((n_rows, wd), U32),
        input_output_aliases={2: 0},
        compiler_params=_cparams("arbitrary"),
    )(dest, hp, xs0)


def _expert_changed(be_ref):
    i = pl.program_id(0)
    return (i == 0) | (be_ref[i] != be_ref[jnp.maximum(i - 1, 0)])


def _expert_hidden_kernel(be_ref, nu_ref, xs_ref, wg_ref, wu_ref, o_ref, wg_bf, wu_bf):
    @pl.when(pl.program_id(0) < nu_ref[0])
    def _():
        @pl.when(_expert_changed(be_ref))
        def _():
            wg_bf[...] = wg_ref[0].astype(BF16)
            wu_bf[...] = wu_ref[0].astype(BF16)

        u = xs_ref[...]
        half = u.shape[1]
        lo = pltpu.bitcast(u << 16, F32).astype(BF16)
        hi = pltpu.bitcast(u & jnp.uint32(0xFFFF0000), F32).astype(BF16)

        def proj(w_bf):
            return (jnp.dot(lo, w_bf[:half, :], preferred_element_type=F32)
                    + jnp.dot(hi, w_bf[half:, :], preferred_element_type=F32))

        gate = proj(wg_bf)
        o_ref[...] = (gate * _sigmoid(gate) * proj(wu_bf)).astype(o_ref.dtype)

    @pl.when(pl.program_id(0) >= nu_ref[0])
    def _():
        o_ref[...] = jnp.zeros_like(o_ref)


def _expert_down_kernel(be_ref, nu_ref, h_ref, wd_ref, o_ref, wd_bf):
    @pl.when(pl.program_id(0) < nu_ref[0])
    def _():
        @pl.when(_expert_changed(be_ref))
        def _():
            wd_bf[...] = wd_ref[0].astype(BF16)

        o_ref[...] = jnp.dot(h_ref[...], wd_bf[...], preferred_element_type=F32)

    @pl.when(pl.program_id(0) >= nu_ref[0])
    def _():
        o_ref[...] = jnp.zeros_like(o_ref)


def _experts(block_e, n_used, xs, w_e_gate, w_e_up, w_e_down, layer):
    n_rows, half = xs.shape
    d = 2 * half
    hid = w_e_gate.shape[2]
    n_blocks = n_rows // MOE_BLOCK
    blk = lambda i, be, nu: jnp.minimum(i, nu[0] - 1)
    expert = lambda i, be, nu: (layer * N_EXPERTS + be[blk(i, be, nu)], 0, 0)
    hidden = pl.pallas_call(
        _expert_hidden_kernel,
        grid_spec=pltpu.PrefetchScalarGridSpec(
            num_scalar_prefetch=2,
            grid=(n_blocks,),
            in_specs=[pl.BlockSpec((MOE_BLOCK, half), lambda i, be, nu: (blk(i, be, nu), 0)),
                      pl.BlockSpec((1, d, hid), expert),
                      pl.BlockSpec((1, d, hid), expert)],
            out_specs=pl.BlockSpec((MOE_BLOCK, hid), lambda i, be, nu: (i, 0)),
            scratch_shapes=[pltpu.VMEM((d, hid), BF16), pltpu.VMEM((d, hid), BF16)],
        ),
        out_shape=jax.ShapeDtypeStruct((n_rows, hid), BF16),
        compiler_params=_cparams("arbitrary"),
    )(block_e, n_used, xs, w_e_gate, w_e_up)
    return pl.pallas_call(
        _expert_down_kernel,
        grid_spec=pltpu.PrefetchScalarGridSpec(
            num_scalar_prefetch=2,
            grid=(n_blocks,),
            in_specs=[pl.BlockSpec((MOE_BLOCK, hid), lambda i, be, nu: (blk(i, be, nu), 0)),
                      pl.BlockSpec((1, hid, d), expert)],
            out_specs=pl.BlockSpec((MOE_BLOCK, d), lambda i, be, nu: (i, 0)),
            scratch_shapes=[pltpu.VMEM((hid, d), BF16)],
        ),
        out_shape=jax.ShapeDtypeStruct((n_rows, d), F32),
        compiler_params=_cparams("arbitrary"),
    )(block_e, n_used, hidden, w_e_down)


def _combine_kernel(dest_ref, x_ref, info_ref, g_ref, y_ref, *rest, tokens_per_step, final):
    if final:
        o_ref, buf, sem = rest
    else:
        x2_ref, o_ref, buf, sem = rest
    step = pl.program_id(0)

    def gather(tile, slot):
        def issue(j, carry):
            tok = tile * tokens_per_step + j
            _row_copy(y_ref, dest_ref[2 * tok], buf.at[slot, 0], j, 1, sem.at[slot]).start()
            _row_copy(y_ref, dest_ref[2 * tok + 1], buf.at[slot, 1], j, 1, sem.at[slot]).start()
            return carry

        lax.fori_loop(0, tokens_per_step, issue, 0)

    @pl.when(step == 0)
    def _():
        gather(0, 0)

    @pl.when(step + 1 < pl.num_programs(0))
    def _():
        gather(step + 1, (step + 1) % 2)

    slot = step % 2
    _row_copy(y_ref, 0, buf.at[slot, 0], 0, tokens_per_step, sem.at[slot]).wait()
    _row_copy(y_ref, 0, buf.at[slot, 1], 0, tokens_per_step, sem.at[slot]).wait()
    info = info_ref[...]
    x2 = x_ref[...] + (buf[slot, 0] * info[:, 4:5] + buf[slot, 1] * info[:, 5:6])
    if not final:
        x2_ref[...] = x2
    o_ref[...] = _rms(x2, g_ref[...]).astype(o_ref.dtype)


def _combine(dest, x, info, g_next, y_rows, final):
    t, d = x.shape
    tc = min(t, 128)
    tile = pl.BlockSpec((tc, d), lambda i, dr: (i, 0))
    out_specs = [tile] if final else [tile, tile]
    out_shape = ([jax.ShapeDtypeStruct((t, d), F32)] if final
                 else [jax.ShapeDtypeStruct((t, d), F32), jax.ShapeDtypeStruct((t, d), BF16)])
    return pl.pallas_call(
        functools.partial(_combine_kernel, tokens_per_step=tc, final=final),
        grid_spec=pltpu.PrefetchScalarGridSpec(
            num_scalar_prefetch=1,
            grid=(t // tc,),
            in_specs=[tile,
                      pl.BlockSpec((tc, LANES), lambda i, dr: (i, 0)),
                      pl.BlockSpec((1, d), lambda i, dr: (0, 0)),
                      pl.BlockSpec(memory_space=pl.ANY)],
            out_specs=out_specs,
            scratch_shapes=[pltpu.VMEM((2, 2, tc, d), F32), pltpu.SemaphoreType.DMA((2,))],
        ),
        out_shape=out_shape,
        compiler_params=_cparams("arbitrary"),
    )(dest, x, info, g_next.reshape(1, d), y_rows)


def _pad_cols(w, width):
    return jnp.pad(w, ((0, 0), (0, width - w.shape[1])))


def _split_w_in(w_in, d):
    dsa_heads = d // (4 * HEAD_DIM)
    hpg = d // 1024
    dil_heads = len(DIL_PATTERNS) * hpg
    a_w, b_w = dsa_heads * HEAD_DIM, dil_heads * HEAD_DIM
    pool_w = d - a_w - b_w
    sizes = (a_w, HEAD_DIM, HEAD_DIM, IDX_HEADS * IDX_DIM, IDX_DIM, IDX_HEADS, b_w, b_w, b_w, pool_w)
    offs = [0]
    for s in sizes:
        offs.append(offs[-1] + s)
    cut = offs[6]
    shift = -cut % LANES
    w_al = jnp.concatenate([w_in[:, :cut].astype(BF16), jnp.zeros((w_in.shape[0], shift), BF16),
                            w_in[:, cut:].astype(BF16)], axis=1)
    w_al = lax.optimization_barrier(w_al)
    moved = lambda n: shift if n >= 6 else 0
    qa, ka, va, qi, ki, wi, qb, kb, vb, c = (w_al[:, offs[n] + moved(n):offs[n + 1] + moved(n)]
                                             for n in range(len(sizes)))
    zero64 = jnp.zeros_like(ki)
    mxu_pad = lambda w: _pad_cols(w, -(-w.shape[1] // MXU_WIDTH) * MXU_WIDTH)
    w_rope = mxu_pad(jnp.concatenate([qa, ka, qb, kb], axis=1))
    w_idx = mxu_pad(jnp.concatenate([qi, ki, zero64, zero64, ki], axis=1))
    w_val = mxu_pad(jnp.concatenate([va, vb], axis=1))
    w_c = mxu_pad(jnp.concatenate([c, wi], axis=1))
    lay = dict(dsa_heads=dsa_heads, dil_heads_per_group=hpg,
               ka=a_w, qb=a_w + HEAD_DIM, kb=a_w + HEAD_DIM + b_w,
               ki_lo=IDX_HEADS * IDX_DIM, ki_hi=IDX_HEADS * IDX_DIM + LANES,
               va=0, vb=HEAD_DIM, wi=pool_w)
    return (w_rope, w_idx, w_val, w_c), lay


def _col_tile(n, cap=512):
    best = LANES
    for m in range(1, cap // LANES + 1):
        if n % (m * LANES) == 0:
            best = m * LANES
    return best


def kernel(x, positions, norm1_g, w_in, w_gate, b_gate, w_br_a, w_br_b, w_br_c, w_pool, pool_scale,
           w_out, norm2_g, w_rg, b_rg, w_re, b_re, w_e_gate, w_e_up, w_e_down, final_g):
    bsz, seq, d = x.shape
    depth = w_in.shape[0]
    t = bsz * seq
    n_blocks = -(-(2 * t) // MOE_BLOCK) + N_EXPERTS
    n_rows = n_blocks * MOE_BLOCK

    c128, s128, c64, s64 = _rope_tables(positions)
    xt = x.reshape(t, d)
    h = _rms_norm(xt, norm1_g[0], BF16)
    out = None
    for l in range(depth):
        (w_rope, w_idx, w_val, w_c), lay = _split_w_in(w_in[l], d)
        zr = _project(h, w_rope, BF16, _col_tile(w_rope.shape[1]), rope=(HEAD_DIM, c128, s128))
        zi = _project(h, w_idx, BF16, _col_tile(w_idx.shape[1]), rope=(IDX_DIM, c64, s64))
        zv = _project(h, w_val, BF16, _col_tile(w_val.shape[1]))
        zc = _project(h, w_c, F32, _col_tile(w_c.shape[1]))

        a_out = _dsa(zr, zi, zv, zc, bsz, seq, lay)
        b_out = _dilated(zr, zv, bsz, seq, lay)
        c_out = _pool(zc, w_pool[l].astype(BF16), pool_scale[l], bsz, seq)

        merged = _merge(h, w_gate[l].astype(BF16), b_gate[l], a_out, b_out, c_out,
                        w_br_a[l].astype(BF16), w_br_b[l].astype(BF16), w_br_c[l].astype(BF16))
        x1 = _outproj(merged, w_out[l].astype(BF16), xt)

        w_router = _pad_cols(jnp.concatenate([w_re[l], w_rg[l]], axis=1), LANES).astype(BF16)
        b_router = _pad_cols(jnp.concatenate([b_re[l], b_rg[l]])[None, :], LANES)
        hp, info, counts = _router(x1, norm2_g[l], w_router, b_router)
        dest2, meta = _plan(counts, info, n_blocks)
        dest = dest2[:, :2].reshape(2 * t)
        block_e = meta[:n_blocks, 0]
        n_used = meta[:1, 1]
        xs = _dispatch(dest, hp, n_rows)
        stack = lambda w: w.reshape((depth * N_EXPERTS,) + w.shape[2:])
        y_rows = _experts(block_e, n_used, xs, stack(w_e_gate), stack(w_e_up), stack(w_e_down), l)
        final = l == depth - 1
        g_next = final_g if final else norm1_g[l + 1]
        res = _combine(dest, x1, info, g_next, y_rows, final)
        if final:
            out = res[0]
        else:
            xt, h = res
    return out.reshape(bsz, seq, d)
```

```python
import functools

import jax
import jax.numpy as jnp
from jax import lax
from jax.experimental import pallas as pl
from jax.experimental.pallas import tpu as pltpu

F32 = jnp.float32
BF16 = jnp.bfloat16
I32 = jnp.int32
U32 = jnp.uint32

LANES = 128
MXU_WIDTH = 256
VMEM_LIMIT = 56 * 1024 * 1024

HEAD_DIM = 128
ROPE_THETA = 500000.0
ROPE_FRACTION = 4
NORM_EPS = 1e-6
IDX_HEADS = 16
IDX_DIM = 64
DSA_TOPK = 256
Q_BLOCK = 128
DIL_PATTERNS = ((128, 1), (512, 4), (2048, 16))
POOL_WINDOWS = (2, 4, 8, 16)
N_EXPERT_GROUPS = 4
EXPERTS_PER_GROUP = 8
N_EXPERTS = N_EXPERT_GROUPS * EXPERTS_PER_GROUP
MOE_BLOCK = 512
NEG_INF = float("-inf")


def _cparams(*sem):
    return pltpu.CompilerParams(dimension_semantics=sem, vmem_limit_bytes=VMEM_LIMIT)


def _rope_table_kernel(pos_ref, f128_ref, s128_ref, f64_ref, s64_ref, c128, sn128, c64, sn64):
    p = pos_ref[...].astype(F32)
    a = p * f128_ref[...]
    c128[...] = jnp.cos(a)
    sn128[...] = jnp.sin(a) * s128_ref[...]
    a = p * f64_ref[...]
    c64[...] = jnp.cos(a)
    sn64[...] = jnp.sin(a) * s64_ref[...]


def _rope_lane_patterns():
    lane = jnp.arange(LANES)

    def pattern(width):
        rd = width // ROPE_FRACTION
        half = rd // 2
        inv = ROPE_THETA ** (-jnp.arange(half, dtype=F32) * (2.0 / rd))
        l = lane % width
        freq = jnp.where(l < rd, inv[l % half], 0.0).astype(F32)
        sign = jnp.where(l < half, -1.0, jnp.where(l < rd, 1.0, 0.0)).astype(F32)
        return freq[None, :], sign[None, :]

    return pattern(HEAD_DIM) + pattern(IDX_DIM)


def _rope_tables(positions):
    t = positions.size
    tm = min(t, 1024)
    f128, s128, f64, s64 = _rope_lane_patterns()
    vec = pl.BlockSpec((1, LANES), lambda i: (0, 0))
    tab = pl.BlockSpec((tm, LANES), lambda i: (i, 0))
    return pl.pallas_call(
        _rope_table_kernel,
        grid=(t // tm,),
        in_specs=[pl.BlockSpec((tm, 1), lambda i: (i, 0)), vec, vec, vec, vec],
        out_specs=[tab] * 4,
        out_shape=[jax.ShapeDtypeStruct((t, LANES), F32)] * 4,
        compiler_params=_cparams("arbitrary"),
    )(positions.reshape(t, 1), f128, s128, f64, s64)


def _rms(x, g):
    ms = jnp.mean(x * x, axis=-1, keepdims=True)
    return x * lax.rsqrt(ms + NORM_EPS) * g


def _norm_kernel(x_ref, g_ref, o_ref):
    o_ref[...] = _rms(x_ref[...], g_ref[...]).astype(o_ref.dtype)


def _rms_norm(x, g, out_dtype):
    t, d = x.shape
    tm = min(t, 512)
    return pl.pallas_call(
        _norm_kernel,
        grid=(t // tm,),
        in_specs=[pl.BlockSpec((tm, d), lambda i: (i, 0)), pl.BlockSpec((1, d), lambda i: (0, 0))],
        out_specs=pl.BlockSpec((tm, d), lambda i: (i, 0)),
        out_shape=jax.ShapeDtypeStruct((t, d), out_dtype),
        compiler_params=_cparams("arbitrary"),
    )(x, g.reshape(1, d))


PROJ_SUB_ROWS = 512


def _proj_kernel(x_ref, w_ref, *rest, rope_width):
    o_ref = rest[-1]
    sub = min(PROJ_SUB_ROWS, x_ref.shape[0])
    for r in range(x_ref.shape[0] // sub):
        rows = pl.ds(r * sub, sub)
        acc = jnp.dot(x_ref[rows, :], w_ref[...], preferred_element_type=F32)
        if rope_width is None:
            o_ref[rows, :] = acc.astype(o_ref.dtype)
            continue
        c_ref, s_ref = rest[:2]
        half = rope_width // ROPE_FRACTION // 2
        c = c_ref[rows, :]
        s = s_ref[rows, :]
        lane = lax.broadcasted_iota(I32, c.shape, 1)
        first_half = (lane % rope_width) < half
        for g in range(acc.shape[1] // LANES):
            z = acc[:, g * LANES:(g + 1) * LANES]
            partner = jnp.where(first_half, pltpu.roll(z, LANES - half, 1), pltpu.roll(z, half, 1))
            o_ref[rows, g * LANES:(g + 1) * LANES] = (z * c + partner * s).astype(o_ref.dtype)


def _project(h, w, out_dtype, tn, rope=None):
    t, k = h.shape
    n = w.shape[1]
    tm = min(t, 2048)
    in_specs = [pl.BlockSpec((tm, k), lambda i, j: (i, 0)), pl.BlockSpec((k, tn), lambda i, j: (0, j))]
    args = [h, w]
    if rope is not None:
        tab = pl.BlockSpec((tm, LANES), lambda i, j: (i, 0))
        in_specs += [tab, tab]
        args += [rope[1], rope[2]]
    return pl.pallas_call(
        functools.partial(_proj_kernel, rope_width=None if rope is None else rope[0]),
        grid=(t // tm, n // tn),
        in_specs=in_specs,
        out_specs=pl.BlockSpec((tm, tn), lambda i, j: (i, j)),
        out_shape=jax.ShapeDtypeStruct((t, n), out_dtype),
        compiler_params=_cparams("arbitrary", "arbitrary"),
    )(*args)


KEY_CHUNK = 256
SUBLANES = 8


def _sublane_allreduce(x, op):
    for shift in (4, 2, 1):
        x = op(x, pltpu.roll(x, shift, 0))
    return x


def _tree_reduce(x, op):
    while x.shape[0] > 1:
        half = x.shape[0] // 2
        x = op(x[:half], x[half:])
    return x[0]


def _vregs(x):
    return x.reshape(x.shape[0] // SUBLANES, SUBLANES, LANES)


def _dsa_kernel(qa_ref, ka_ref, va_ref, qi_ref, kil_ref, kih_ref, wi_ref, o_ref,
                key_ref, vat_ref, acc_ref, m_ref, l_ref, *, topk, n_heads):
    nq = o_ref.shape[0]
    ck = KEY_CHUNK
    i = pl.program_id(1)
    t0 = i * nq
    n_chunks = (t0 + nq + ck - 1) // ck
    nt = (((1,), (1,)), ((), ()))

    @pl.when(i == 0)
    def _():
        for c in range(vat_ref.shape[0]):
            vat_ref[c] = va_ref[c * ck:(c + 1) * ck, :].astype(F32).T.astype(BF16)

    def causal_mask(c):
        shape = (ck // SUBLANES, SUBLANES, nq)
        spos = (c * ck + lax.broadcasted_iota(I32, shape, 0) * SUBLANES
                + lax.broadcasted_iota(I32, shape, 1))
        tpos = t0 + lax.broadcasted_iota(I32, shape, 2)
        return spos <= tpos

    wt = (wi_ref[...] * ((IDX_DIM ** -0.5) * (IDX_HEADS ** -0.5))).T
    w8 = [jnp.broadcast_to(wt[h:h + 1, :], (SUBLANES, LANES)) for h in range(IDX_HEADS)]
    q_stack = [jnp.concatenate([qi_ref[:, (2 * m) * LANES:(2 * m + 1) * LANES],
                                qi_ref[:, (2 * m + 1) * LANES:(2 * m + 2) * LANES]], axis=0)
               for m in range(IDX_HEADS // 4)]

    def score_chunk(c, carry):
        r0 = pl.multiple_of(c * ck, ck)
        kil = kil_ref[pl.ds(r0, ck), :]
        kih = kih_ref[pl.ds(r0, ck), :]
        acc = jnp.zeros((ck // SUBLANES, SUBLANES, LANES), F32)
        for m in range(IDX_HEADS // 4):
            d_even = lax.dot_general(kil, q_stack[m], nt, preferred_element_type=F32)
            d_odd = lax.dot_general(kih, q_stack[m], nt, preferred_element_type=F32)
            for dots, h in ((d_even[:, :nq], 4 * m), (d_odd[:, :nq], 4 * m + 1),
                            (d_even[:, nq:], 4 * m + 2), (d_odd[:, nq:], 4 * m + 3)):
                acc = acc + jnp.maximum(_vregs(dots), 0.0) * w8[h][None]
        score = jnp.where(causal_mask(c), acc, NEG_INF).reshape(ck, nq)
        bits = pltpu.bitcast(score, I32)
        key_ref[c] = bits ^ ((bits >> 31) & jnp.int32(0x7FFFFFFF))
        return carry

    lax.fori_loop(0, n_chunks, score_chunk, 0)

    def count(compare, cand):
        def body(c, cnt):
            return cnt + _tree_reduce(jnp.where(compare(_vregs(key_ref[c]), cand[None]), 1.0, 0.0), jnp.add)
        cnt = lax.fori_loop(0, n_chunks, body, jnp.zeros((SUBLANES, LANES), F32))
        return _sublane_allreduce(cnt, jnp.add)

    count_ge = functools.partial(count, jnp.greater_equal)
    int_min = jnp.int32(-2 ** 31)
    thr = jnp.where(count_ge(jnp.zeros((SUBLANES, LANES), I32)) >= topk, jnp.int32(0), int_min)

    def refine(it, thr):
        cand = thr | (jnp.int32(1) << (30 - it))
        return jnp.where(count_ge(cand) >= topk, cand, thr)

    thr = lax.fori_loop(0, 31, refine, thr)
    tie_slots = topk - count(jnp.greater, thr)
    r = lax.broadcasted_iota(I32, (ck, ck), 0)
    cidx = lax.broadcasted_iota(I32, (ck, ck), 1)
    earlier = jnp.where(cidx < r, 1.0, 0.0).astype(BF16)

    qt = jnp.concatenate([qa_ref[:, h * HEAD_DIM:(h + 1) * HEAD_DIM].astype(F32).T for h in range(n_heads)],
                         axis=1).astype(BF16)
    m_ref[...] = jnp.full(m_ref.shape, NEG_INF, F32)
    l_ref[...] = jnp.zeros(l_ref.shape, F32)
    acc_ref[...] = jnp.zeros(acc_ref.shape, F32)

    def attend(c, ties_seen):
        r0 = pl.multiple_of(c * ck, ck)
        key = _vregs(key_ref[c])
        tie = jnp.where(key == thr[None], 1.0, 0.0)
        ties_before = _vregs(jnp.dot(earlier, tie.reshape(ck, nq).astype(BF16),
                                     preferred_element_type=F32)) + ties_seen[None]
        tie_bias = jnp.where(ties_before < tie_slots[None], tie - 1.0, NEG_INF)
        bias = jnp.where(key > thr[None], 0.0, jnp.where(tie > 0.0, tie_bias, NEG_INF))
        bias = jnp.where(causal_mask(c), bias, NEG_INF)
        ties_seen = ties_seen + _sublane_allreduce(_tree_reduce(tie, jnp.add), jnp.add)
        s_all = jnp.dot(ka_ref[pl.ds(r0, ck), :], qt, preferred_element_type=F32) * (HEAD_DIM ** -0.5)
        ps, alphas = [], []
        for h in range(n_heads):
            s = _vregs(s_all[:, h * nq:(h + 1) * nq]) + bias
            m_old = m_ref[h]
            m_new = jnp.maximum(m_old, _sublane_allreduce(_tree_reduce(s, jnp.maximum), jnp.maximum))
            m_safe = jnp.where(m_new == NEG_INF, 0.0, m_new)
            alphas.append(jnp.exp(m_old - m_safe))
            p = jnp.exp(s - m_safe[None])
            l_ref[h] = alphas[h] * l_ref[h] + _tree_reduce(p, jnp.add)
            ps.append(p.reshape(ck, nq).astype(BF16))
            m_ref[h] = m_new
        pv = jnp.dot(vat_ref[c], jnp.concatenate(ps, axis=1), preferred_element_type=F32)
        for h in range(n_heads):
            acc_ref[h] = (_vregs(acc_ref[h]) * alphas[h][None]).reshape(HEAD_DIM, nq) + pv[:, h * nq:(h + 1) * nq]
        return ties_seen

    lax.fori_loop(0, n_chunks, attend, jnp.zeros((SUBLANES, LANES), F32))
    for h in range(n_heads):
        l = _sublane_allreduce(l_ref[h], jnp.add)
        o = (_vregs(acc_ref[h]) / l[None]).reshape(HEAD_DIM, nq)
        o_ref[:, h * HEAD_DIM:(h + 1) * HEAD_DIM] = o.T.astype(o_ref.dtype)


def _dsa(zr, zi, zv, zc, bsz, seq, lay):
    t = bsz * seq
    nqb = seq // Q_BLOCK
    n_heads = lay["dsa_heads"]
    a_w = n_heads * HEAD_DIM
    qw = IDX_HEADS * IDX_DIM
    nck = seq // KEY_CHUNK
    row = lambda b, i: b * nqb + i
    in_specs = [
        pl.BlockSpec((Q_BLOCK, a_w), lambda b, i: (row(b, i), 0)),
        pl.BlockSpec((seq, HEAD_DIM), lambda b, i: (b, lay["ka"] // HEAD_DIM)),
        pl.BlockSpec((seq, HEAD_DIM), lambda b, i: (b, lay["va"] // HEAD_DIM)),
        pl.BlockSpec((Q_BLOCK, qw), lambda b, i: (row(b, i), 0)),
        pl.BlockSpec((seq, LANES), lambda b, i: (b, lay["ki_lo"] // LANES)),
        pl.BlockSpec((seq, LANES), lambda b, i: (b, lay["ki_hi"] // LANES)),
        pl.BlockSpec((Q_BLOCK, LANES), lambda b, i: (row(b, i), lay["wi"] // LANES)),
    ]
    return pl.pallas_call(
        functools.partial(_dsa_kernel, topk=min(DSA_TOPK, seq // 4), n_heads=n_heads),
        grid=(bsz, nqb),
        in_specs=in_specs,
        out_specs=pl.BlockSpec((Q_BLOCK, a_w), lambda b, i: (row(b, i), 0)),
        out_shape=jax.ShapeDtypeStruct((t, a_w), BF16),
        scratch_shapes=[pltpu.VMEM((nck, KEY_CHUNK, Q_BLOCK), I32),
                        pltpu.VMEM((nck, HEAD_DIM, KEY_CHUNK), BF16),
                        pltpu.VMEM((n_heads, HEAD_DIM, Q_BLOCK), F32),
                        pltpu.VMEM((n_heads, SUBLANES, Q_BLOCK), F32),
                        pltpu.VMEM((n_heads, SUBLANES, Q_BLOCK), F32)],
        compiler_params=_cparams("arbitrary", "arbitrary"),
    )(zr, zr, zv, zi, zi, zi, zc)


MIX_ROWS = 256


def _band_attention(q, k_blocks, v_blocks):
    bs = q.shape[0]
    nt = (((1,), (1,)), ((), ()))
    k = k_blocks[0] if len(k_blocks) == 1 else jnp.concatenate(k_blocks, axis=0)
    v = v_blocks[0] if len(v_blocks) == 1 else jnp.concatenate(v_blocks, axis=0)
    s = lax.dot_general(q, k, nt, preferred_element_type=F32) * (HEAD_DIM ** -0.5)
    a = lax.broadcasted_iota(I32, s.shape, 0)
    c = lax.broadcasted_iota(I32, s.shape, 1)
    if len(k_blocks) == 1:
        valid = c <= a
    else:
        valid = (a + bs - c).astype(U32) <= jnp.uint32(bs)
    s = jnp.where(valid, s, NEG_INF)
    m = jnp.max(s, axis=1, keepdims=True)
    p = jnp.exp(s - m)
    l = jnp.sum(p, axis=1, keepdims=True)
    o = jnp.dot(p.astype(BF16), v, preferred_element_type=F32) / l
    return o, m + jnp.log(l)


def _dil_kernel(*refs, seq):
    ng = len(DIL_PATTERNS)
    q_refs, k_refs, v_refs = refs[:ng], refs[ng:2 * ng], refs[2 * ng:3 * ng]
    o_ref, qf, kf, vf, of, lf = refs[3 * ng:]
    for g, (window, dil) in enumerate(DIL_PATTERNS):
        ls = seq // dil
        bs = min(ls, Q_BLOCK)
        assert ls % bs == 0 and window // dil == Q_BLOCK
        if dil > 1:
            qf[...] = q_refs[g][...].astype(F32)
            kf[...] = k_refs[g][...].astype(F32)
            vf[...] = v_refs[g][...].astype(F32)
        for x in range(dil):
            def rows(n):
                return pl.ds(x + dil * bs * n, bs, stride=dil) if dil > 1 else pl.ds(bs * n, bs)

            def load(src_bf16, src_f32, n):
                return src_bf16[rows(n), :] if dil == 1 else src_f32[rows(n), :].astype(BF16)

            k_prev = v_prev = None
            for n in range(ls // bs):
                q = load(q_refs[g], qf, n)
                k_cur = load(k_refs[g], kf, n)
                v_cur = load(v_refs[g], vf, n)
                if n == 0:
                    o, lse = _band_attention(q, [k_cur], [v_cur])
                else:
                    o, lse = _band_attention(q, [k_prev, k_cur], [v_prev, v_cur])
                of[g, rows(n), :] = o
                lf[g, rows(n), :] = jnp.broadcast_to(lse, o.shape)
                k_prev, v_prev = k_cur, v_cur
    mr = min(MIX_ROWS, seq)
    for r in range(seq // mr):
        sl = pl.ds(r * mr, mr)
        lses = [lf[g, sl, :] for g in range(ng)]
        mx = functools.reduce(jnp.maximum, lses)
        es = [jnp.exp(x - mx) for x in lses]
        tot = functools.reduce(lambda u, w: u + w, es)
        acc = functools.reduce(lambda u, w: u + w, [(e / tot) * of[g, sl, :] for g, e in enumerate(es)])
        o_ref[sl, :] = acc.astype(o_ref.dtype)


def _dilated(zr, zv, bsz, seq, lay):
    t = bsz * seq
    hpg = lay["dil_heads_per_group"]
    ng = len(DIL_PATTERNS)

    def head_spec(base, g):
        return pl.BlockSpec((seq, HEAD_DIM), lambda b, s: (b, base // HEAD_DIM + g * hpg + s))

    in_specs = ([head_spec(lay["qb"], g) for g in range(ng)]
                + [head_spec(lay["kb"], g) for g in range(ng)]
                + [head_spec(lay["vb"], g) for g in range(ng)])
    return pl.pallas_call(
        functools.partial(_dil_kernel, seq=seq),
        grid=(bsz, hpg),
        in_specs=in_specs,
        out_specs=pl.BlockSpec((seq, HEAD_DIM), lambda b, s: (b, s)),
        out_shape=jax.ShapeDtypeStruct((t, hpg * HEAD_DIM), BF16),
        scratch_shapes=[pltpu.VMEM((seq, HEAD_DIM), F32)] * 3
                       + [pltpu.VMEM((ng, seq, HEAD_DIM), F32)] * 2,
        compiler_params=_cparams("arbitrary", "arbitrary"),
    )(*([zr] * (2 * ng) + [zv] * ng))


def _pool_kernel(c_ref, wp_ref, ps_ref, o_ref):
    x = c_ref[...]
    row = lax.broadcasted_iota(I32, x.shape, 0)

    def run(window):
        acc = x
        k = 1
        while k < window:
            acc = acc + jnp.where(row >= k, pltpu.roll(acc, k, 0), 0.0)
            k *= 2
        denom = jnp.minimum(row[:, :1] + 1, window).astype(F32)
        y = (acc / denom - x).astype(BF16)
        o = jnp.dot(y, wp_ref[0], preferred_element_type=F32) * ps_ref[...]
        o_ref[...] = o.astype(o_ref.dtype)

    for gi, window in enumerate(POOL_WINDOWS):
        pl.when(pl.program_id(1) == gi)(functools.partial(run, window))


def _pool(zc, w_pool, pool_scale, bsz, seq):
    t = bsz * seq
    ng, gw = w_pool.shape[0], w_pool.shape[1]
    return pl.pallas_call(
        _pool_kernel,
        grid=(bsz, ng),
        in_specs=[pl.BlockSpec((seq, gw), lambda b, g: (b, g)),
                  pl.BlockSpec((1, gw, gw), lambda b, g: (g, 0, 0)),
                  pl.BlockSpec((1, gw), lambda b, g: (0, g))],
        out_specs=pl.BlockSpec((seq, gw), lambda b, g: (b, g)),
        out_shape=jax.ShapeDtypeStruct((t, ng * gw), BF16),
        compiler_params=_cparams("arbitrary", "arbitrary"),
    )(zc, w_pool, pool_scale.reshape(1, ng * gw))


def _sigmoid(x):
    return 1.0 / (1.0 + jnp.exp(-x))


def _merge_kernel(h_ref, wg0, wg1, wg2, bg0, bg1, bg2, a_ref, b_ref, c_ref, wa, wb, wc, o_ref):
    h = h_ref[...]
    acc = None
    for wg, bg, br, wbr in ((wg0, bg0, a_ref, wa), (wg1, bg1, b_ref, wb), (wg2, bg2, c_ref, wc)):
        gate = _sigmoid(jnp.dot(h, wg[...], preferred_element_type=F32) + bg[...])
        term = gate * jnp.dot(br[...], wbr[...], preferred_element_type=F32)
        acc = term if acc is None else acc + term
    o_ref[...] = acc.astype(o_ref.dtype)


def _merge(h, w_gate, b_gate, a_out, b_out, c_out, w_a, w_b, w_c):
    t, d = h.shape
    tm = min(t, 1024)
    tn = 256
    nd = d // tn
    rows = lambda width: pl.BlockSpec((tm, width), lambda i, j: (i, 0))
    cols = lambda k, off: pl.BlockSpec((k, tn), lambda i, j: (0, off * nd + j))
    in_specs = ([rows(d)] + [cols(d, n) for n in range(3)] + [cols(1, n) for n in range(3)]
                + [rows(a_out.shape[1]), rows(b_out.shape[1]), rows(c_out.shape[1])]
                + [cols(a_out.shape[1], 0), cols(b_out.shape[1], 0), cols(c_out.shape[1], 0)])
    bg = b_gate.reshape(1, 3 * d)
    return pl.pallas_call(
        _merge_kernel,
        grid=(t // tm, nd),
        in_specs=in_specs,
        out_specs=pl.BlockSpec((tm, tn), lambda i, j: (i, j)),
        out_shape=jax.ShapeDtypeStruct((t, d), BF16),
        compiler_params=_cparams("arbitrary", "arbitrary"),
    )(h, w_gate, w_gate, w_gate, bg, bg, bg, a_out, b_out, c_out, w_a, w_b, w_c)


def _outproj_kernel(m_ref, w_ref, x_ref, o_ref):
    o_ref[...] = x_ref[...] + jnp.dot(m_ref[...], w_ref[...], preferred_element_type=F32)


def _outproj(merged, w_out, x):
    t, d = x.shape
    tm = min(t, 1024)
    tn = 512
    return pl.pallas_call(
        _outproj_kernel,
        grid=(t // tm, d // tn),
        in_specs=[pl.BlockSpec((tm, d), lambda i, j: (i, 0)),
                  pl.BlockSpec((d, tn), lambda i, j: (0, j)),
                  pl.BlockSpec((tm, tn), lambda i, j: (i, j))],
        out_specs=pl.BlockSpec((tm, tn), lambda i, j: (i, j)),
        out_shape=jax.ShapeDtypeStruct((t, d), F32),
        compiler_params=_cparams("arbitrary", "arbitrary"),
    )(merged, w_out, x)


GROUP_LANE0 = N_EXPERTS


def _lane_min_index(mask, lane):
    return jnp.min(jnp.where(mask, lane, LANES), axis=1, keepdims=True)


def _router_kernel(x_ref, g_ref, wr_ref, br_ref, hp_ref, info_ref, cnt_ref, carry_ref):
    @pl.when(pl.program_id(0) == 0)
    def _():
        carry_ref[...] = jnp.zeros_like(carry_ref)

    hb = _rms(x_ref[...], g_ref[...]).astype(BF16)
    tm, d = hb.shape
    lo = pltpu.bitcast(hb[:, :d // 2].astype(F32), U32)
    hi = pltpu.bitcast(hb[:, d // 2:].astype(F32), U32)
    hp_ref[...] = hi | (lo >> 16)

    logits = jnp.dot(hb, wr_ref[...], preferred_element_type=F32) + br_ref[...]
    lane = lax.broadcasted_iota(I32, logits.shape, 1)
    is_group = (lane >= GROUP_LANE0) & (lane < GROUP_LANE0 + N_EXPERT_GROUPS)
    lg = jnp.where(is_group, logits, NEG_INF)
    gmax = jnp.max(lg, axis=1, keepdims=True)
    gsel = _lane_min_index(lg == gmax, lane) - GROUP_LANE0
    pg_sel = 1.0 / jnp.sum(jnp.where(is_group, jnp.exp(lg - gmax), 0.0), axis=1, keepdims=True)
    in_group = (lane < N_EXPERTS) & ((lane // EXPERTS_PER_GROUP) == gsel)
    le = jnp.where(in_group, logits, NEG_INF)
    ee = jnp.where(in_group, jnp.exp(le - jnp.max(le, axis=1, keepdims=True)), 0.0)
    pe = jnp.where(in_group, ee / jnp.sum(ee, axis=1, keepdims=True), -1.0)
    v0 = jnp.max(pe, axis=1, keepdims=True)
    e0 = _lane_min_index(pe == v0, lane)
    pe1 = jnp.where(lane == e0, -1.0, pe)
    v1 = jnp.max(pe1, axis=1, keepdims=True)
    e1 = _lane_min_index(pe1 == v1, lane)
    w0 = pg_sel * v0 / (v0 + v1)
    w1 = pg_sel * v1 / (v0 + v1)
    onehot = jnp.where((lane == e0) | (lane == e1), 1.0, 0.0)
    r = lax.broadcasted_iota(I32, (tm, tm), 0)
    c = lax.broadcasted_iota(I32, (tm, tm), 1)
    before = jnp.where(c < r, 1.0, 0.0).astype(BF16)
    prior = jnp.dot(before, onehot.astype(BF16), preferred_element_type=F32) + carry_ref[...]
    r0 = jnp.sum(jnp.where(lane == e0, prior, 0.0), axis=1, keepdims=True)
    r1 = jnp.sum(jnp.where(lane == e1, prior, 0.0), axis=1, keepdims=True)
    carry_ref[...] = carry_ref[...] + jnp.sum(onehot, axis=0, keepdims=True)
    cnt_ref[...] = carry_ref[...]
    info = jnp.zeros_like(logits)
    for k, val in enumerate((e0.astype(F32), e1.astype(F32), r0, r1, w0, w1)):
        info = jnp.where(lane == k, val, info)
    info_ref[...] = info


def _router(x, g, w_router, b_router):
    t, d = x.shape
    tm = min(t, 256)
    return pl.pallas_call(
        _router_kernel,
        grid=(t // tm,),
        in_specs=[pl.BlockSpec((tm, d), lambda i: (i, 0)),
                  pl.BlockSpec((1, d), lambda i: (0, 0)),
                  pl.BlockSpec((d, LANES), lambda i: (0, 0)),
                  pl.BlockSpec((1, LANES), lambda i: (0, 0))],
        out_specs=[pl.BlockSpec((tm, d // 2), lambda i: (i, 0)),
                   pl.BlockSpec((tm, LANES), lambda i: (i, 0)),
                   pl.BlockSpec((1, LANES), lambda i: (0, 0))],
        out_shape=[jax.ShapeDtypeStruct((t, d // 2), U32),
                   jax.ShapeDtypeStruct((t, LANES), F32),
                   jax.ShapeDtypeStruct((1, LANES), F32)],
        scratch_shapes=[pltpu.VMEM((1, LANES), F32)],
        compiler_params=_cparams("arbitrary"),
    )(x, g.reshape(1, d), w_router, b_router)


def _plan_kernel(cnt_ref, info_ref, dest_ref, meta_ref):
    lane = lax.broadcasted_iota(I32, (8, LANES), 1)
    cnt = jnp.broadcast_to(cnt_ref[...], (8, LANES))
    nblk = jnp.floor((cnt + (MOE_BLOCK - 1)) * (1.0 / MOE_BLOCK))
    r = lax.broadcasted_iota(I32, (LANES, LANES), 0)
    c = lax.broadcasted_iota(I32, (LANES, LANES), 1)
    upto = jnp.where(r <= c, 1.0, 0.0).astype(BF16)
    end_blk = jnp.dot(nblk.astype(BF16), upto, preferred_element_type=F32)
    start_row = (end_blk - nblk) * MOE_BLOCK

    info = info_ref[...]
    tl = lax.broadcasted_iota(I32, info.shape, 1)
    e0, e1, r0, r1 = (info[:, k:k + 1] for k in range(4))
    start = start_row[:1, :]
    d0 = jnp.sum(jnp.where(tl == e0.astype(I32), start, 0.0), axis=1, keepdims=True) + r0
    d1 = jnp.sum(jnp.where(tl == e1.astype(I32), start, 0.0), axis=1, keepdims=True) + r1
    dest_ref[...] = jnp.where(tl == 0, d0, jnp.where(tl == 1, d1, 0.0)).astype(I32)

    @pl.when(pl.program_id(0) == 0)
    def _():
        nb = meta_ref.shape[0]
        blk = lax.broadcasted_iota(I32, (nb, LANES), 0).astype(F32)
        ml = lax.broadcasted_iota(I32, (nb, LANES), 1)
        done = jnp.where((ml < N_EXPERTS) & (end_blk[:1, :] <= blk), 1.0, 0.0)
        block_e = jnp.minimum(jnp.sum(done, axis=1, keepdims=True), N_EXPERTS - 1.0)
        n_used = jnp.sum(jnp.where(lane[:1, :] == N_EXPERTS - 1, end_blk[:1, :], 0.0), axis=1, keepdims=True)
        meta_ref[...] = jnp.where(ml == 0, block_e, jnp.where(ml == 1, n_used, 0.0)).astype(I32)


def _plan(counts, info, n_blocks):
    t = info.shape[0]
    tm = min(t, 1024)
    nb = -(-n_blocks // 8) * 8
    return pl.pallas_call(
        _plan_kernel,
        grid=(t // tm,),
        in_specs=[pl.BlockSpec((1, LANES), lambda i: (0, 0)), pl.BlockSpec((tm, LANES), lambda i: (i, 0))],
        out_specs=[pl.BlockSpec((tm, LANES), lambda i: (i, 0)), pl.BlockSpec((nb, LANES), lambda i: (0, 0))],
        out_shape=[jax.ShapeDtypeStruct((t, LANES), I32), jax.ShapeDtypeStruct((nb, LANES), I32)],
        compiler_params=_cparams("arbitrary"),
    )(counts, info)


def _row_copy(src, src_row, dst, dst_row, n, sem):
    return pltpu.make_async_copy(src.at[pl.ds(src_row, n)], dst.at[pl.ds(dst_row, n)], sem)


def _dispatch_kernel(dest_ref, hp_ref, xs_in_ref, xs_ref, sem, *, tokens_per_step):
    del xs_in_ref
    base = pl.program_id(0) * tokens_per_step

    def issue(j, carry):
        tok = base + j
        _row_copy(hp_ref, j, xs_ref, dest_ref[2 * tok], 1, sem).start()
        _row_copy(hp_ref, j, xs_ref, dest_ref[2 * tok + 1], 1, sem).start()
        return carry

    lax.fori_loop(0, tokens_per_step, issue, 0)
    _row_copy(hp_ref, 0, xs_ref, 0, tokens_per_step, sem).wait()
    _row_copy(hp_ref, 0, xs_ref, 0, tokens_per_step, sem).wait()


def _dispatch(dest, hp, n_rows):
    t, wd = hp.shape
    tc = min(t, 512)
    xs0 = jnp.zeros((n_rows, wd), U32)
    return pl.pallas_call(
        functools.partial(_dispatch_kernel, tokens_per_step=tc),
        grid_spec=pltpu.PrefetchScalarGridSpec(
            num_scalar_prefetch=1,
            grid=(t // tc,),
            in_specs=[pl.BlockSpec((tc, wd), lambda i, dr: (i, 0)), pl.BlockSpec(memory_space=pl.ANY)],
            out_specs=pl.BlockSpec(memory_space=pl.ANY),
            scratch_shapes=[pltpu.SemaphoreType.DMA(())],
        ),
        out_shape=jax.ShapeDtypeStruct((n_rows, wd), U32),
        input_output_aliases={2: 0},
        compiler_params=_cparams("arbitrary"),
    )(dest, hp, xs0)


def _expert_changed(be_ref):
    i = pl.program_id(0)
    return (i == 0) | (be_ref[i] != be_ref[jnp.maximum(i - 1, 0)])


def _expert_hidden_kernel(be_ref, nu_ref, xs_ref, wg_ref, wu_ref, o_ref, wg_bf, wu_bf):
    @pl.when(pl.program_id(0) < nu_ref[0])
    def _():
        @pl.when(_expert_changed(be_ref))
        def _():
            wg_bf[...] = wg_ref[0].astype(BF16)
            wu_bf[...] = wu_ref[0].astype(BF16)

        u = xs_ref[...]
        half = u.shape[1]
        lo = pltpu.bitcast(u << 16, F32).astype(BF16)
        hi = pltpu.bitcast(u & jnp.uint32(0xFFFF0000), F32).astype(BF16)

        def proj(w_bf):
            return (jnp.dot(lo, w_bf[:half, :], preferred_element_type=F32)
                    + jnp.dot(hi, w_bf[half:, :], preferred_element_type=F32))

        gate = proj(wg_bf)
        o_ref[...] = (gate * _sigmoid(gate) * proj(wu_bf)).astype(o_ref.dtype)

    @pl.when(pl.program_id(0) >= nu_ref[0])
    def _():
        o_ref[...] = jnp.zeros_like(o_ref)


def _expert_down_kernel(be_ref, nu_ref, h_ref, wd_ref, o_ref, wd_bf):
    @pl.when(pl.program_id(0) < nu_ref[0])
    def _():
        @pl.when(_expert_changed(be_ref))
        def _():
            wd_bf[...] = wd_ref[0].astype(BF16)

        o_ref[...] = jnp.dot(h_ref[...], wd_bf[...], preferred_element_type=F32)

    @pl.when(pl.program_id(0) >= nu_ref[0])
    def _():
        o_ref[...] = jnp.zeros_like(o_ref)


def _experts(block_e, n_used, xs, w_e_gate, w_e_up, w_e_down, layer):
    n_rows, half = xs.shape
    d = 2 * half
    hid = w_e_gate.shape[2]
    n_blocks = n_rows // MOE_BLOCK
    blk = lambda i, be, nu: jnp.minimum(i, nu[0] - 1)
    expert = lambda i, be, nu: (layer * N_EXPERTS + be[blk(i, be, nu)], 0, 0)
    hidden = pl.pallas_call(
        _expert_hidden_kernel,
        grid_spec=pltpu.PrefetchScalarGridSpec(
            num_scalar_prefetch=2,
            grid=(n_blocks,),
            in_specs=[pl.BlockSpec((MOE_BLOCK, half), lambda i, be, nu: (blk(i, be, nu), 0)),
                      pl.BlockSpec((1, d, hid), expert),
                      pl.BlockSpec((1, d, hid), expert)],
            out_specs=pl.BlockSpec((MOE_BLOCK, hid), lambda i, be, nu: (i, 0)),
            scratch_shapes=[pltpu.VMEM((d, hid), BF16), pltpu.VMEM((d, hid), BF16)],
        ),
        out_shape=jax.ShapeDtypeStruct((n_rows, hid), BF16),
        compiler_params=_cparams("arbitrary"),
    )(block_e, n_used, xs, w_e_gate, w_e_up)
    return pl.pallas_call(
        _expert_down_kernel,
        grid_spec=pltpu.PrefetchScalarGridSpec(
            num_scalar_prefetch=2,
            grid=(n_blocks,),
            in_specs=[pl.BlockSpec((MOE_BLOCK, hid), lambda i, be, nu: (blk(i, be, nu), 0)),
                      pl.BlockSpec((1, hid, d), expert)],
            out_specs=pl.BlockSpec((MOE_BLOCK, d), lambda i, be, nu: (i, 0)),
            scratch_shapes=[pltpu.VMEM((hid, d), BF16)],
        ),
        out_shape=jax.ShapeDtypeStruct((n_rows, d), F32),
        compiler_params=_cparams("arbitrary"),
    )(block_e, n_used, hidden, w_e_down)


def _combine_kernel(dest_ref, x_ref, info_ref, g_ref, y_ref, *rest, tokens_per_step, final):
    if final:
        o_ref, buf, sem = rest
    else:
        x2_ref, o_ref, buf, sem = rest
    step = pl.program_id(0)

    def gather(tile, slot):
        def issue(j, carry):
            tok = tile * tokens_per_step + j
            _row_copy(y_ref, dest_ref[2 * tok], buf.at[slot, 0], j, 1, sem.at[slot]).start()
            _row_copy(y_ref, dest_ref[2 * tok + 1], buf.at[slot, 1], j, 1, sem.at[slot]).start()
            return carry

        lax.fori_loop(0, tokens_per_step, issue, 0)

    @pl.when(step == 0)
    def _():
        gather(0, 0)

    @pl.when(step + 1 < pl.num_programs(0))
    def _():
        gather(step + 1, (step + 1) % 2)

    slot = step % 2
    _row_copy(y_ref, 0, buf.at[slot, 0], 0, tokens_per_step, sem.at[slot]).wait()
    _row_copy(y_ref, 0, buf.at[slot, 1], 0, tokens_per_step, sem.at[slot]).wait()
    info = info_ref[...]
    x2 = x_ref[...] + (buf[slot, 0] * info[:, 4:5] + buf[slot, 1] * info[:, 5:6])
    if not final:
        x2_ref[...] = x2
    o_ref[...] = _rms(x2, g_ref[...]).astype(o_ref.dtype)


def _combine(dest, x, info, g_next, y_rows, final):
    t, d = x.shape
    tc = min(t, 256)
    tile = pl.BlockSpec((tc, d), lambda i, dr: (i, 0))
    out_specs = [tile] if final else [tile, tile]
    out_shape = ([jax.ShapeDtypeStruct((t, d), F32)] if final
                 else [jax.ShapeDtypeStruct((t, d), F32), jax.ShapeDtypeStruct((t, d), BF16)])
    return pl.pallas_call(
        functools.partial(_combine_kernel, tokens_per_step=tc, final=final),
        grid_spec=pltpu.PrefetchScalarGridSpec(
            num_scalar_prefetch=1,
            grid=(t // tc,),
            in_specs=[tile,
                      pl.BlockSpec((tc, LANES), lambda i, dr: (i, 0)),
                      pl.BlockSpec((1, d), lambda i, dr: (0, 0)),
                      pl.BlockSpec(memory_space=pl.ANY)],
            out_specs=out_specs,
            scratch_shapes=[pltpu.VMEM((2, 2, tc, d), F32), pltpu.SemaphoreType.DMA((2,))],
        ),
        out_shape=out_shape,
        compiler_params=_cparams("arbitrary"),
    )(dest, x, info, g_next.reshape(1, d), y_rows)


def _pad_cols(w, width):
    return jnp.pad(w, ((0, 0), (0, width - w.shape[1])))


def _split_w_in(w_in, d):
    dsa_heads = d // (4 * HEAD_DIM)
    hpg = d // 1024
    dil_heads = len(DIL_PATTERNS) * hpg
    a_w, b_w = dsa_heads * HEAD_DIM, dil_heads * HEAD_DIM
    pool_w = d - a_w - b_w
    sizes = (a_w, HEAD_DIM, HEAD_DIM, IDX_HEADS * IDX_DIM, IDX_DIM, IDX_HEADS, b_w, b_w, b_w, pool_w)
    offs = [0]
    for s in sizes:
        offs.append(offs[-1] + s)
    cut = offs[6]
    shift = -cut % LANES
    w_al = jnp.concatenate([w_in[:, :cut].astype(BF16), jnp.zeros((w_in.shape[0], shift), BF16),
                            w_in[:, cut:].astype(BF16)], axis=1)
    w_al = lax.optimization_barrier(w_al)
    moved = lambda n: shift if n >= 6 else 0
    qa, ka, va, qi, ki, wi, qb, kb, vb, c = (w_al[:, offs[n] + moved(n):offs[n + 1] + moved(n)]
                                             for n in range(len(sizes)))
    zero64 = jnp.zeros_like(ki)
    mxu_pad = lambda w: _pad_cols(w, -(-w.shape[1] // MXU_WIDTH) * MXU_WIDTH)
    w_rope = mxu_pad(jnp.concatenate([qa, ka, qb, kb], axis=1))
    w_idx = mxu_pad(jnp.concatenate([qi, ki, zero64, zero64, ki], axis=1))
    w_val = mxu_pad(jnp.concatenate([va, vb], axis=1))
    w_c = mxu_pad(jnp.concatenate([c, wi], axis=1))
    lay = dict(dsa_heads=dsa_heads, dil_heads_per_group=hpg,
               ka=a_w, qb=a_w + HEAD_DIM, kb=a_w + HEAD_DIM + b_w,
               ki_lo=IDX_HEADS * IDX_DIM, ki_hi=IDX_HEADS * IDX_DIM + LANES,
               va=0, vb=HEAD_DIM, wi=pool_w)
    return (w_rope, w_idx, w_val, w_c), lay


def _col_tile(n, cap=512):
    best = LANES
    for m in range(1, cap // LANES + 1):
        if n % (m * LANES) == 0:
            best = m * LANES
    return best


def kernel(x, positions, norm1_g, w_in, w_gate, b_gate, w_br_a, w_br_b, w_br_c, w_pool, pool_scale,
           w_out, norm2_g, w_rg, b_rg, w_re, b_re, w_e_gate, w_e_up, w_e_down, final_g):
    bsz, seq, d = x.shape
    depth = w_in.shape[0]
    t = bsz * seq
    n_blocks = -(-(2 * t) // MOE_BLOCK) + N_EXPERTS
    n_rows = n_blocks * MOE_BLOCK

    c128, s128, c64, s64 = _rope_tables(positions)
    xt = x.reshape(t, d)
    h = _rms_norm(xt, norm1_g[0], BF16)
    out = None
    for l in range(depth):
        (w_rope, w_idx, w_val, w_c), lay = _split_w_in(w_in[l], d)
        zr = _project(h, w_rope, BF16, _col_tile(w_rope.shape[1]), rope=(HEAD_DIM, c128, s128))
        zi = _project(h, w_idx, BF16, _col_tile(w_idx.shape[1]), rope=(IDX_DIM, c64, s64))
        zv = _project(h, w_val, BF16, _col_tile(w_val.shape[1]))
        zc = _project(h, w_c, F32, _col_tile(w_c.shape[1]))

        a_out = _dsa(zr, zi, zv, zc, bsz, seq, lay)
        b_out = _dilated(zr, zv, bsz, seq, lay)
        c_out = _pool(zc, w_pool[l].astype(BF16), pool_scale[l], bsz, seq)

        merged = _merge(h, w_gate[l].astype(BF16), b_gate[l], a_out, b_out, c_out,
                        w_br_a[l].astype(BF16), w_br_b[l].astype(BF16), w_br_c[l].astype(BF16))
        x1 = _outproj(merged, w_out[l].astype(BF16), xt)

        w_router = _pad_cols(jnp.concatenate([w_re[l], w_rg[l]], axis=1), LANES).astype(BF16)
        b_router = _pad_cols(jnp.concatenate([b_re[l], b_rg[l]])[None, :], LANES)
        hp, info, counts = _router(x1, norm2_g[l], w_router, b_router)
        dest2, meta = _plan(counts, info, n_blocks)
        dest = dest2[:, :2].reshape(2 * t)
        block_e = meta[:n_blocks, 0]
        n_used = meta[:1, 1]
        xs = _dispatch(dest, hp, n_rows)
        stack = lambda w: w.reshape((depth * N_EXPERTS,) + w.shape[2:])
        y_rows = _experts(block_e, n_used, xs, stack(w_e_gate), stack(w_e_up), stack(w_e_down), l)
        final = l == depth - 1
        g_next = final_g if final else norm1_g[l + 1]
        res = _combine(dest, x1, info, g_next, y_rows, final)
        if final:
            out = res[0]
        else:
            xt, h = res
    return out.reshape(bsz, seq, d)
```

```python
import functools

import jax
import jax.numpy as jnp
from jax import lax
from jax.experimental import pallas as pl
from jax.experimental.pallas import tpu as pltpu

F32 = jnp.float32
BF16 = jnp.bfloat16
I32 = jnp.int32
U32 = jnp.uint32

LANES = 128
MXU_WIDTH = 256
VMEM_LIMIT = 56 * 1024 * 1024

HEAD_DIM = 128
ROPE_THETA = 500000.0
ROPE_FRACTION = 4
NORM_EPS = 1e-6
IDX_HEADS = 16
IDX_DIM = 64
DSA_TOPK = 256
Q_BLOCK = 128
DIL_PATTERNS = ((128, 1), (512, 4), (2048, 16))
POOL_WINDOWS = (2, 4, 8, 16)
N_EXPERT_GROUPS = 4
EXPERTS_PER_GROUP = 8
N_EXPERTS = N_EXPERT_GROUPS * EXPERTS_PER_GROUP
MOE_BLOCK = 512
NEG_INF = float("-inf")


def _cparams(*sem):
    return pltpu.CompilerParams(dimension_semantics=sem, vmem_limit_bytes=VMEM_LIMIT)


def _rope_table_kernel(pos_ref, f128_ref, s128_ref, f64_ref, s64_ref, c128, sn128, c64, sn64):
    p = pos_ref[...].astype(F32)
    a = p * f128_ref[...]
    c128[...] = jnp.cos(a)
    sn128[...] = jnp.sin(a) * s128_ref[...]
    a = p * f64_ref[...]
    c64[...] = jnp.cos(a)
    sn64[...] = jnp.sin(a) * s64_ref[...]


def _rope_lane_patterns():
    lane = jnp.arange(LANES)

    def pattern(width):
        rd = width // ROPE_FRACTION
        half = rd // 2
        inv = ROPE_THETA ** (-jnp.arange(half, dtype=F32) * (2.0 / rd))
        l = lane % width
        freq = jnp.where(l < rd, inv[l % half], 0.0).astype(F32)
        sign = jnp.where(l < half, -1.0, jnp.where(l < rd, 1.0, 0.0)).astype(F32)
        return freq[None, :], sign[None, :]

    return pattern(HEAD_DIM) + pattern(IDX_DIM)


def _rope_tables(positions):
    t = positions.size
    tm = min(t, 1024)
    f128, s128, f64, s64 = _rope_lane_patterns()
    vec = pl.BlockSpec((1, LANES), lambda i: (0, 0))
    tab = pl.BlockSpec((tm, LANES), lambda i: (i, 0))
    return pl.pallas_call(
        _rope_table_kernel,
        grid=(t // tm,),
        in_specs=[pl.BlockSpec((tm, 1), lambda i: (i, 0)), vec, vec, vec, vec],
        out_specs=[tab] * 4,
        out_shape=[jax.ShapeDtypeStruct((t, LANES), F32)] * 4,
        compiler_params=_cparams("arbitrary"),
    )(positions.reshape(t, 1), f128, s128, f64, s64)


def _rms(x, g):
    ms = jnp.mean(x * x, axis=-1, keepdims=True)
    return x * lax.rsqrt(ms + NORM_EPS) * g


def _norm_kernel(x_ref, g_ref, o_ref):
    o_ref[...] = _rms(x_ref[...], g_ref[...]).astype(o_ref.dtype)


def _rms_norm(x, g, out_dtype):
    t, d = x.shape
    tm = min(t, 512)
    return pl.pallas_call(
        _norm_kernel,
        grid=(t // tm,),
        in_specs=[pl.BlockSpec((tm, d), lambda i: (i, 0)), pl.BlockSpec((1, d), lambda i: (0, 0))],
        out_specs=pl.BlockSpec((tm, d), lambda i: (i, 0)),
        out_shape=jax.ShapeDtypeStruct((t, d), out_dtype),
        compiler_params=_cparams("arbitrary"),
    )(x, g.reshape(1, d))


PROJ_SUB_ROWS = 512


def _proj_kernel(tiles_ref, x_ref, w_ref, *rest, rope_width):
    del tiles_ref
    o_ref = rest[-1]
    sub = min(PROJ_SUB_ROWS, x_ref.shape[0])
    for r in range(x_ref.shape[0] // sub):
        rows = pl.ds(r * sub, sub)
        acc = jnp.dot(x_ref[rows, :], w_ref[...], preferred_element_type=F32)
        if rope_width is None:
            o_ref[rows, :] = acc.astype(o_ref.dtype)
            continue
        c_ref, s_ref = rest[:2]
        half = rope_width // ROPE_FRACTION // 2
        c = c_ref[rows, :]
        s = s_ref[rows, :]
        lane = lax.broadcasted_iota(I32, c.shape, 1)
        first_half = (lane % rope_width) < half
        for g in range(acc.shape[1] // LANES):
            z = acc[:, g * LANES:(g + 1) * LANES]
            partner = jnp.where(first_half, pltpu.roll(z, LANES - half, 1), pltpu.roll(z, half, 1))
            o_ref[rows, g * LANES:(g + 1) * LANES] = (z * c + partner * s).astype(o_ref.dtype)


def _project(h, w, tiles, out_dtype, rope=None):
    t, k = h.shape
    tn = MXU_WIDTH
    tm = min(t, 2048)
    in_specs = [pl.BlockSpec((tm, k), lambda i, j, tl: (i, 0)),
                pl.BlockSpec((k, tn), lambda i, j, tl: (0, tl[j]))]
    args = [h, w]
    if rope is not None:
        tab = pl.BlockSpec((tm, LANES), lambda i, j, tl: (i, 0))
        in_specs += [tab, tab]
        args += [rope[1], rope[2]]
    return pl.pallas_call(
        functools.partial(_proj_kernel, rope_width=None if rope is None else rope[0]),
        grid_spec=pltpu.PrefetchScalarGridSpec(
            num_scalar_prefetch=1,
            grid=(t // tm, len(tiles)),
            in_specs=in_specs,
            out_specs=pl.BlockSpec((tm, tn), lambda i, j, tl: (i, j)),
        ),
        out_shape=jax.ShapeDtypeStruct((t, len(tiles) * tn), out_dtype),
        compiler_params=_cparams("arbitrary", "arbitrary"),
    )(jnp.asarray(tiles, I32), *args)


KEY_CHUNK = 256
SUBLANES = 8


def _sublane_allreduce(x, op):
    for shift in (4, 2, 1):
        x = op(x, pltpu.roll(x, shift, 0))
    return x


def _tree_reduce(x, op):
    while x.shape[0] > 1:
        half = x.shape[0] // 2
        x = op(x[:half], x[half:])
    return x[0]


def _vregs(x):
    return x.reshape(x.shape[0] // SUBLANES, SUBLANES, LANES)


def _dsa_kernel(qa_ref, ka_ref, va_ref, qi_ref, ki_ref, wi_ref, o_ref,
                key_ref, vat_ref, kil_ref, kih_ref, acc_ref, m_ref, l_ref, *, topk, n_heads, wi_lane):
    nq = o_ref.shape[0]
    ck = KEY_CHUNK
    i = pl.program_id(1)
    t0 = i * nq
    n_chunks = (t0 + nq + ck - 1) // ck
    nt = (((1,), (1,)), ((), ()))

    @pl.when(i == 0)
    def _():
        for c in range(vat_ref.shape[0]):
            vat_ref[c] = va_ref[c * ck:(c + 1) * ck, :].astype(F32).T.astype(BF16)
        ki = ki_ref[...].astype(F32)
        ki = jnp.where(lax.broadcasted_iota(I32, ki.shape, 1) < IDX_DIM, ki, 0.0)
        kil_ref[...] = ki.astype(BF16)
        kih_ref[...] = pltpu.roll(ki, IDX_DIM, 1).astype(BF16)

    def causal_mask(c):
        shape = (ck // SUBLANES, SUBLANES, nq)
        spos = (c * ck + lax.broadcasted_iota(I32, shape, 0) * SUBLANES
                + lax.broadcasted_iota(I32, shape, 1))
        tpos = t0 + lax.broadcasted_iota(I32, shape, 2)
        return spos <= tpos

    wt = (wi_ref[...] * ((IDX_DIM ** -0.5) * (IDX_HEADS ** -0.5))).T
    w8 = [jnp.broadcast_to(wt[wi_lane + h:wi_lane + h + 1, :], (SUBLANES, LANES)) for h in range(IDX_HEADS)]
    q_stack = [jnp.concatenate([qi_ref[:, (2 * m) * LANES:(2 * m + 1) * LANES],
                                qi_ref[:, (2 * m + 1) * LANES:(2 * m + 2) * LANES]], axis=0)
               for m in range(IDX_HEADS // 4)]

    def score_chunk(c, carry):
        r0 = pl.multiple_of(c * ck, ck)
        kil = kil_ref[pl.ds(r0, ck), :]
        kih = kih_ref[pl.ds(r0, ck), :]
        acc = jnp.zeros((ck // SUBLANES, SUBLANES, LANES), F32)
        for m in range(IDX_HEADS // 4):
            d_even = lax.dot_general(kil, q_stack[m], nt, preferred_element_type=F32)
            d_odd = lax.dot_general(kih, q_stack[m], nt, preferred_element_type=F32)
            for dots, h in ((d_even[:, :nq], 4 * m), (d_odd[:, :nq], 4 * m + 1),
                            (d_even[:, nq:], 4 * m + 2), (d_odd[:, nq:], 4 * m + 3)):
                acc = acc + jnp.maximum(_vregs(dots), 0.0) * w8[h][None]
        score = jnp.where(causal_mask(c), acc, NEG_INF).reshape(ck, nq)
        bits = pltpu.bitcast(score, I32)
        key_ref[c] = bits ^ ((bits >> 31) & jnp.int32(0x7FFFFFFF))
        return carry

    lax.fori_loop(0, n_chunks, score_chunk, 0)

    def count(compare, cand):
        def body(c, cnt):
            return cnt + _tree_reduce(jnp.where(compare(_vregs(key_ref[c]), cand[None]), 1.0, 0.0), jnp.add)
        cnt = lax.fori_loop(0, n_chunks, body, jnp.zeros((SUBLANES, LANES), F32))
        return _sublane_allreduce(cnt, jnp.add)

    count_ge = functools.partial(count, jnp.greater_equal)
    int_min = jnp.int32(-2 ** 31)
    thr = jnp.where(count_ge(jnp.zeros((SUBLANES, LANES), I32)) >= topk, jnp.int32(0), int_min)

    def refine(it, thr):
        cand = thr | (jnp.int32(1) << (30 - it))
        return jnp.where(count_ge(cand) >= topk, cand, thr)

    thr = lax.fori_loop(0, 31, refine, thr)
    tie_slots = topk - count(jnp.greater, thr)
    r = lax.broadcasted_iota(I32, (ck, ck), 0)
    cidx = lax.broadcasted_iota(I32, (ck, ck), 1)
    earlier = jnp.where(cidx < r, 1.0, 0.0).astype(BF16)

    qt = jnp.concatenate([qa_ref[:, h * HEAD_DIM:(h + 1) * HEAD_DIM].astype(F32).T for h in range(n_heads)],
                         axis=1).astype(BF16)
    m_ref[...] = jnp.full(m_ref.shape, NEG_INF, F32)
    l_ref[...] = jnp.zeros(l_ref.shape, F32)
    acc_ref[...] = jnp.zeros(acc_ref.shape, F32)

    def attend(c, ties_seen):
        r0 = pl.multiple_of(c * ck, ck)
        key = _vregs(key_ref[c])
        tie = jnp.where(key == thr[None], 1.0, 0.0)
        ties_before = _vregs(jnp.dot(earlier, tie.reshape(ck, nq).astype(BF16),
                                     preferred_element_type=F32)) + ties_seen[None]
        tie_bias = jnp.where(ties_before < tie_slots[None], tie - 1.0, NEG_INF)
        bias = jnp.where(key > thr[None], 0.0, jnp.where(tie > 0.0, tie_bias, NEG_INF))
        bias = jnp.where(causal_mask(c), bias, NEG_INF)
        ties_seen = ties_seen + _sublane_allreduce(_tree_reduce(tie, jnp.add), jnp.add)
        s_all = jnp.dot(ka_ref[pl.ds(r0, ck), :], qt, preferred_element_type=F32) * (HEAD_DIM ** -0.5)
        ps, alphas = [], []
        for h in range(n_heads):
            s = _vregs(s_all[:, h * nq:(h + 1) * nq]) + bias
            m_old = m_ref[h]
            m_new = jnp.maximum(m_old, _sublane_allreduce(_tree_reduce(s, jnp.maximum), jnp.maximum))
            m_safe = jnp.where(m_new == NEG_INF, 0.0, m_new)
            alphas.append(jnp.exp(m_old - m_safe))
            p = jnp.exp(s - m_safe[None])
            l_ref[h] = alphas[h] * l_ref[h] + _tree_reduce(p, jnp.add)
            ps.append(p.reshape(ck, nq).astype(BF16))
            m_ref[h] = m_new
        pv = jnp.dot(vat_ref[c], jnp.concatenate(ps, axis=1), preferred_element_type=F32)
        for h in range(n_heads):
            acc_ref[h] = (_vregs(acc_ref[h]) * alphas[h][None]).reshape(HEAD_DIM, nq) + pv[:, h * nq:(h + 1) * nq]
        return ties_seen

    lax.fori_loop(0, n_chunks, attend, jnp.zeros((SUBLANES, LANES), F32))
    for h in range(n_heads):
        l = _sublane_allreduce(l_ref[h], jnp.add)
        o = (_vregs(acc_ref[h]) / l[None]).reshape(HEAD_DIM, nq)
        o_ref[:, h * HEAD_DIM:(h + 1) * HEAD_DIM] = o.T.astype(o_ref.dtype)


def _dsa(zr, zi, zv, zc, bsz, seq, lay):
    t = bsz * seq
    nqb = seq // Q_BLOCK
    n_heads = lay["dsa_heads"]
    a_w = n_heads * HEAD_DIM
    qw = IDX_HEADS * IDX_DIM
    nck = seq // KEY_CHUNK
    row = lambda b, i: b * nqb + i
    in_specs = [
        pl.BlockSpec((Q_BLOCK, a_w), lambda b, i: (row(b, i), 0)),
        pl.BlockSpec((seq, HEAD_DIM), lambda b, i: (b, lay["ka"] // HEAD_DIM)),
        pl.BlockSpec((seq, HEAD_DIM), lambda b, i: (b, lay["va"] // HEAD_DIM)),
        pl.BlockSpec((Q_BLOCK, qw), lambda b, i: (row(b, i), 0)),
        pl.BlockSpec((seq, LANES), lambda b, i: (b, lay["ki"] // LANES)),
        pl.BlockSpec((Q_BLOCK, LANES), lambda b, i: (row(b, i), lay["wi"] // LANES)),
    ]
    return pl.pallas_call(
        functools.partial(_dsa_kernel, topk=min(DSA_TOPK, seq // 4), n_heads=n_heads, wi_lane=lay["wi_lane"]),
        grid=(bsz, nqb),
        in_specs=in_specs,
        out_specs=pl.BlockSpec((Q_BLOCK, a_w), lambda b, i: (row(b, i), 0)),
        out_shape=jax.ShapeDtypeStruct((t, a_w), BF16),
        scratch_shapes=[pltpu.VMEM((nck, KEY_CHUNK, Q_BLOCK), I32),
                        pltpu.VMEM((nck, HEAD_DIM, KEY_CHUNK), BF16),
                        pltpu.VMEM((seq, LANES), BF16),
                        pltpu.VMEM((seq, LANES), BF16),
                        pltpu.VMEM((n_heads, HEAD_DIM, Q_BLOCK), F32),
                        pltpu.VMEM((n_heads, SUBLANES, Q_BLOCK), F32),
                        pltpu.VMEM((n_heads, SUBLANES, Q_BLOCK), F32)],
        compiler_params=_cparams("arbitrary", "arbitrary"),
    )(zr, zr, zv, zi, zi, zc)


MIX_ROWS = 256


def _band_attention(q, k_blocks, v_blocks):
    bs = q.shape[0]
    nt = (((1,), (1,)), ((), ()))
    k = k_blocks[0] if len(k_blocks) == 1 else jnp.concatenate(k_blocks, axis=0)
    v = v_blocks[0] if len(v_blocks) == 1 else jnp.concatenate(v_blocks, axis=0)
    s = lax.dot_general(q, k, nt, preferred_element_type=F32) * (HEAD_DIM ** -0.5)
    a = lax.broadcasted_iota(I32, s.shape, 0)
    c = lax.broadcasted_iota(I32, s.shape, 1)
    if len(k_blocks) == 1:
        valid = c <= a
    else:
        valid = (a + bs - c).astype(U32) <= jnp.uint32(bs)
    s = jnp.where(valid, s, NEG_INF)
    m = jnp.max(s, axis=1, keepdims=True)
    p = jnp.exp(s - m)
    l = jnp.sum(p, axis=1, keepdims=True)
    o = jnp.dot(p.astype(BF16), v, preferred_element_type=F32) / l
    return o, m + jnp.log(l)


def _dil_kernel(*refs, seq):
    ng = len(DIL_PATTERNS)
    q_refs, k_refs, v_refs = refs[:ng], refs[ng:2 * ng], refs[2 * ng:3 * ng]
    o_ref, qf, kf, vf, of, lf = refs[3 * ng:]
    for g, (window, dil) in enumerate(DIL_PATTERNS):
        ls = seq // dil
        bs = min(ls, Q_BLOCK)
        assert ls % bs == 0 and window // dil == Q_BLOCK
        if dil > 1:
            qf[...] = q_refs[g][...].astype(F32)
            kf[...] = k_refs[g][...].astype(F32)
            vf[...] = v_refs[g][...].astype(F32)
        for x in range(dil):
            def rows(n):
                return pl.ds(x + dil * bs * n, bs, stride=dil) if dil > 1 else pl.ds(bs * n, bs)

            def load(src_bf16, src_f32, n):
                return src_bf16[rows(n), :] if dil == 1 else src_f32[rows(n), :].astype(BF16)

            k_prev = v_prev = None
            for n in range(ls // bs):
                q = load(q_refs[g], qf, n)
                k_cur = load(k_refs[g], kf, n)
                v_cur = load(v_refs[g], vf, n)
                if n == 0:
                    o, lse = _band_attention(q, [k_cur], [v_cur])
                else:
                    o, lse = _band_attention(q, [k_prev, k_cur], [v_prev, v_cur])
                of[g, rows(n), :] = o
                lf[g, rows(n), :] = jnp.broadcast_to(lse, o.shape)
                k_prev, v_prev = k_cur, v_cur
    mr = min(MIX_ROWS, seq)
    for r in range(seq // mr):
        sl = pl.ds(r * mr, mr)
        lses = [lf[g, sl, :] for g in range(ng)]
        mx = functools.reduce(jnp.maximum, lses)
        es = [jnp.exp(x - mx) for x in lses]
        tot = functools.reduce(lambda u, w: u + w, es)
        acc = functools.reduce(lambda u, w: u + w, [(e / tot) * of[g, sl, :] for g, e in enumerate(es)])
        o_ref[sl, :] = acc.astype(o_ref.dtype)


def _dilated(zr, zv, bsz, seq, lay):
    t = bsz * seq
    hpg = lay["dil_heads_per_group"]
    ng = len(DIL_PATTERNS)

    def head_spec(base, g):
        return pl.BlockSpec((seq, HEAD_DIM), lambda b, s: (b, base // HEAD_DIM + g * hpg + s))

    in_specs = ([head_spec(lay["qb"], g) for g in range(ng)]
                + [head_spec(lay["kb"], g) for g in range(ng)]
                + [head_spec(lay["vb"], g) for g in range(ng)])
    return pl.pallas_call(
        functools.partial(_dil_kernel, seq=seq),
        grid=(bsz, hpg),
        in_specs=in_specs,
        out_specs=pl.BlockSpec((seq, HEAD_DIM), lambda b, s: (b, s)),
        out_shape=jax.ShapeDtypeStruct((t, hpg * HEAD_DIM), BF16),
        scratch_shapes=[pltpu.VMEM((seq, HEAD_DIM), F32)] * 3
                       + [pltpu.VMEM((ng, seq, HEAD_DIM), F32)] * 2,
        compiler_params=_cparams("arbitrary", "arbitrary"),
    )(*([zr] * (2 * ng) + [zv] * ng))


def _pool_kernel(c_ref, wp_ref, ps_ref, o_ref):
    x = c_ref[...]
    row = lax.broadcasted_iota(I32, x.shape, 0)

    def run(window):
        acc = x
        k = 1
        while k < window:
            acc = acc + jnp.where(row >= k, pltpu.roll(acc, k, 0), 0.0)
            k *= 2
        denom = jnp.minimum(row[:, :1] + 1, window).astype(F32)
        y = (acc / denom - x).astype(BF16)
        o = jnp.dot(y, wp_ref[0], preferred_element_type=F32) * ps_ref[...]
        o_ref[...] = o.astype(o_ref.dtype)

    for gi, window in enumerate(POOL_WINDOWS):
        pl.when(pl.program_id(1) == gi)(functools.partial(run, window))


def _pool(zc, w_pool, pool_scale, bsz, seq):
    t = bsz * seq
    ng, gw = w_pool.shape[0], w_pool.shape[1]
    return pl.pallas_call(
        _pool_kernel,
        grid=(bsz, ng),
        in_specs=[pl.BlockSpec((seq, gw), lambda b, g: (b, g)),
                  pl.BlockSpec((1, gw, gw), lambda b, g: (g, 0, 0)),
                  pl.BlockSpec((1, gw), lambda b, g: (0, g))],
        out_specs=pl.BlockSpec((seq, gw), lambda b, g: (b, g)),
        out_shape=jax.ShapeDtypeStruct((t, ng * gw), BF16),
        compiler_params=_cparams("arbitrary", "arbitrary"),
    )(zc, w_pool, pool_scale.reshape(1, ng * gw))


def _sigmoid(x):
    return 1.0 / (1.0 + jnp.exp(-x))


MERGE_SUB_ROWS = 512


def _merge_kernel(h_ref, wg0, wg1, wg2, bg0, bg1, bg2, a_ref, b_ref, c_ref, wa, wb, wc, o_ref):
    sub = min(MERGE_SUB_ROWS, h_ref.shape[0])
    for r in range(h_ref.shape[0] // sub):
        rows = pl.ds(r * sub, sub)
        h = h_ref[rows, :]
        acc = None
        for wg, bg, br, wbr in ((wg0, bg0, a_ref, wa), (wg1, bg1, b_ref, wb), (wg2, bg2, c_ref, wc)):
            gate = _sigmoid(jnp.dot(h, wg[...], preferred_element_type=F32) + bg[...])
            term = gate * jnp.dot(br[rows, :], wbr[...], preferred_element_type=F32)
            acc = term if acc is None else acc + term
        o_ref[rows, :] = acc.astype(o_ref.dtype)


def _merge(h, w_gate, b_gate, a_out, b_out, c_out, w_a, w_b, w_c):
    t, d = h.shape
    tm = min(t, 1024)
    tn = 256
    nd = d // tn
    rows = lambda width: pl.BlockSpec((tm, width), lambda i, j: (i, 0))
    cols = lambda k, off: pl.BlockSpec((k, tn), lambda i, j: (0, off * nd + j))
    in_specs = ([rows(d)] + [cols(d, n) for n in range(3)] + [cols(1, n) for n in range(3)]
                + [rows(a_out.shape[1]), rows(b_out.shape[1]), rows(c_out.shape[1])]
                + [cols(a_out.shape[1], 0), cols(b_out.shape[1], 0), cols(c_out.shape[1], 0)])
    bg = b_gate.reshape(1, 3 * d)
    return pl.pallas_call(
        _merge_kernel,
        grid=(t // tm, nd),
        in_specs=in_specs,
        out_specs=pl.BlockSpec((tm, tn), lambda i, j: (i, j)),
        out_shape=jax.ShapeDtypeStruct((t, d), BF16),
        compiler_params=_cparams("arbitrary", "arbitrary"),
    )(h, w_gate, w_gate, w_gate, bg, bg, bg, a_out, b_out, c_out, w_a, w_b, w_c)


def _outproj_kernel(m_ref, w_ref, x_ref, o_ref):
    sub = min(MERGE_SUB_ROWS, m_ref.shape[0])
    for r in range(m_ref.shape[0] // sub):
        rows = pl.ds(r * sub, sub)
        o_ref[rows, :] = x_ref[rows, :] + jnp.dot(m_ref[rows, :], w_ref[...], preferred_element_type=F32)


def _outproj(merged, w_out, x):
    t, d = x.shape
    tm = min(t, 1024)
    tn = 512
    return pl.pallas_call(
        _outproj_kernel,
        grid=(t // tm, d // tn),
        in_specs=[pl.BlockSpec((tm, d), lambda i, j: (i, 0)),
                  pl.BlockSpec((d, tn), lambda i, j: (0, j)),
                  pl.BlockSpec((tm, tn), lambda i, j: (i, j))],
        out_specs=pl.BlockSpec((tm, tn), lambda i, j: (i, j)),
        out_shape=jax.ShapeDtypeStruct((t, d), F32),
        compiler_params=_cparams("arbitrary", "arbitrary"),
    )(merged, w_out, x)


GROUP_LANE0 = N_EXPERTS


def _lane_min_index(mask, lane):
    return jnp.min(jnp.where(mask, lane, LANES), axis=1, keepdims=True)


def _router_kernel(x_ref, g_ref, wr_ref, br_ref, hp_ref, info_ref, cnt_ref, carry_ref):
    @pl.when(pl.program_id(0) == 0)
    def _():
        carry_ref[...] = jnp.zeros_like(carry_ref)

    hb = _rms(x_ref[...], g_ref[...]).astype(BF16)
    tm, d = hb.shape
    lo = pltpu.bitcast(hb[:, :d // 2].astype(F32), U32)
    hi = pltpu.bitcast(hb[:, d // 2:].astype(F32), U32)
    hp_ref[...] = hi | (lo >> 16)

    logits = jnp.dot(hb, wr_ref[...], preferred_element_type=F32) + br_ref[...]
    lane = lax.broadcasted_iota(I32, logits.shape, 1)
    is_group = (lane >= GROUP_LANE0) & (lane < GROUP_LANE0 + N_EXPERT_GROUPS)
    lg = jnp.where(is_group, logits, NEG_INF)
    gmax = jnp.max(lg, axis=1, keepdims=True)
    gsel = _lane_min_index(lg == gmax, lane) - GROUP_LANE0
    pg_sel = 1.0 / jnp.sum(jnp.where(is_group, jnp.exp(lg - gmax), 0.0), axis=1, keepdims=True)
    in_group = (lane < N_EXPERTS) & ((lane // EXPERTS_PER_GROUP) == gsel)
    le = jnp.where(in_group, logits, NEG_INF)
    ee = jnp.where(in_group, jnp.exp(le - jnp.max(le, axis=1, keepdims=True)), 0.0)
    pe = jnp.where(in_group, ee / jnp.sum(ee, axis=1, keepdims=True), -1.0)
    v0 = jnp.max(pe, axis=1, keepdims=True)
    e0 = _lane_min_index(pe == v0, lane)
    pe1 = jnp.where(lane == e0, -1.0, pe)
    v1 = jnp.max(pe1, axis=1, keepdims=True)
    e1 = _lane_min_index(pe1 == v1, lane)
    w0 = pg_sel * v0 / (v0 + v1)
    w1 = pg_sel * v1 / (v0 + v1)
    onehot = jnp.where((lane == e0) | (lane == e1), 1.0, 0.0)
    r = lax.broadcasted_iota(I32, (tm, tm), 0)
    c = lax.broadcasted_iota(I32, (tm, tm), 1)
    before = jnp.where(c < r, 1.0, 0.0).astype(BF16)
    prior = jnp.dot(before, onehot.astype(BF16), preferred_element_type=F32) + carry_ref[...]
    r0 = jnp.sum(jnp.where(lane == e0, prior, 0.0), axis=1, keepdims=True)
    r1 = jnp.sum(jnp.where(lane == e1, prior, 0.0), axis=1, keepdims=True)
    carry_ref[...] = carry_ref[...] + jnp.sum(onehot, axis=0, keepdims=True)
    cnt_ref[...] = carry_ref[...]
    info = jnp.zeros_like(logits)
    for k, val in enumerate((e0.astype(F32), e1.astype(F32), r0, r1, w0, w1)):
        info = jnp.where(lane == k, val, info)
    info_ref[...] = info


def _router(x, g, w_router, b_router):
    t, d = x.shape
    tm = min(t, 256)
    return pl.pallas_call(
        _router_kernel,
        grid=(t // tm,),
        in_specs=[pl.BlockSpec((tm, d), lambda i: (i, 0)),
                  pl.BlockSpec((1, d), lambda i: (0, 0)),
                  pl.BlockSpec((d, LANES), lambda i: (0, 0)),
                  pl.BlockSpec((1, LANES), lambda i: (0, 0))],
        out_specs=[pl.BlockSpec((tm, d // 2), lambda i: (i, 0)),
                   pl.BlockSpec((tm, LANES), lambda i: (i, 0)),
                   pl.BlockSpec((1, LANES), lambda i: (0, 0))],
        out_shape=[jax.ShapeDtypeStruct((t, d // 2), U32),
                   jax.ShapeDtypeStruct((t, LANES), F32),
                   jax.ShapeDtypeStruct((1, LANES), F32)],
        scratch_shapes=[pltpu.VMEM((1, LANES), F32)],
        compiler_params=_cparams("arbitrary"),
    )(x, g.reshape(1, d), w_router, b_router)


def _plan_kernel(cnt_ref, info_ref, dest_ref, meta_ref):
    lane = lax.broadcasted_iota(I32, (8, LANES), 1)
    cnt = jnp.broadcast_to(cnt_ref[...], (8, LANES))
    nblk = jnp.floor((cnt + (MOE_BLOCK - 1)) * (1.0 / MOE_BLOCK))
    r = lax.broadcasted_iota(I32, (LANES, LANES), 0)
    c = lax.broadcasted_iota(I32, (LANES, LANES), 1)
    upto = jnp.where(r <= c, 1.0, 0.0).astype(BF16)
    end_blk = jnp.dot(nblk.astype(BF16), upto, preferred_element_type=F32)
    start_row = (end_blk - nblk) * MOE_BLOCK

    info = info_ref[...]
    tl = lax.broadcasted_iota(I32, info.shape, 1)
    e0, e1, r0, r1 = (info[:, k:k + 1] for k in range(4))
    start = start_row[:1, :]
    d0 = jnp.sum(jnp.where(tl == e0.astype(I32), start, 0.0), axis=1, keepdims=True) + r0
    d1 = jnp.sum(jnp.where(tl == e1.astype(I32), start, 0.0), axis=1, keepdims=True) + r1
    dest_ref[...] = jnp.where(tl == 0, d0, jnp.where(tl == 1, d1, 0.0)).astype(I32)

    @pl.when(pl.program_id(0) == 0)
    def _():
        nb = meta_ref.shape[0]
        blk = lax.broadcasted_iota(I32, (nb, LANES), 0).astype(F32)
        ml = lax.broadcasted_iota(I32, (nb, LANES), 1)
        done = jnp.where((ml < N_EXPERTS) & (end_blk[:1, :] <= blk), 1.0, 0.0)
        block_e = jnp.minimum(jnp.sum(done, axis=1, keepdims=True), N_EXPERTS - 1.0)
        n_used = jnp.sum(jnp.where(lane[:1, :] == N_EXPERTS - 1, end_blk[:1, :], 0.0), axis=1, keepdims=True)
        meta_ref[...] = jnp.where(ml == 0, block_e, jnp.where(ml == 1, n_used, 0.0)).astype(I32)


def _plan(counts, info, n_blocks):
    t = info.shape[0]
    tm = min(t, 1024)
    nb = -(-n_blocks // 8) * 8
    return pl.pallas_call(
        _plan_kernel,
        grid=(t // tm,),
        in_specs=[pl.BlockSpec((1, LANES), lambda i: (0, 0)), pl.BlockSpec((tm, LANES), lambda i: (i, 0))],
        out_specs=[pl.BlockSpec((tm, LANES), lambda i: (i, 0)), pl.BlockSpec((nb, LANES), lambda i: (0, 0))],
        out_shape=[jax.ShapeDtypeStruct((t, LANES), I32), jax.ShapeDtypeStruct((nb, LANES), I32)],
        compiler_params=_cparams("arbitrary"),
    )(counts, info)


def _row_copy(src, src_row, dst, dst_row, n, sem):
    return pltpu.make_async_copy(src.at[pl.ds(src_row, n)], dst.at[pl.ds(dst_row, n)], sem)


def _dispatch_kernel(dest_ref, hp_ref, xs_in_ref, xs_ref, sem, *, tokens_per_step):
    del xs_in_ref
    base = pl.program_id(0) * tokens_per_step

    def issue(j, carry):
        tok = base + j
        _row_copy(hp_ref, j, xs_ref, dest_ref[2 * tok], 1, sem).start()
        _row_copy(hp_ref, j, xs_ref, dest_ref[2 * tok + 1], 1, sem).start()
        return carry

    lax.fori_loop(0, tokens_per_step, issue, 0)
    _row_copy(hp_ref, 0, xs_ref, 0, tokens_per_step, sem).wait()
    _row_copy(hp_ref, 0, xs_ref, 0, tokens_per_step, sem).wait()


def _dispatch(dest, hp, n_rows):
    t, wd = hp.shape
    tc = min(t, 512)
    xs0 = jnp.zeros((n_rows, wd), U32)
    return pl.pallas_call(
        functools.partial(_dispatch_kernel, tokens_per_step=tc),
        grid_spec=pltpu.PrefetchScalarGridSpec(
            num_scalar_prefetch=1,
            grid=(t // tc,),
            in_specs=[pl.BlockSpec((tc, wd), lambda i, dr: (i, 0)), pl.BlockSpec(memory_space=pl.ANY)],
            out_specs=pl.BlockSpec(memory_space=pl.ANY),
            scratch_shapes=[pltpu.SemaphoreType.DMA(())],
        ),
        out_shape=jax.ShapeDtypeStruct((n_rows, wd), U32),
        input_output_aliases={2: 0},
        compiler_params=_cparams("arbitrary"),
    )(dest, hp, xs0)


def _expert_changed(be_ref):
    i = pl.program_id(0)
    return (i == 0) | (be_ref[i] != be_ref[jnp.maximum(i - 1, 0)])


def _expert_hidden_kernel(be_ref, nu_ref, xs_ref, wg_ref, wu_ref, o_ref, wg_bf, wu_bf):
    @pl.when(pl.program_id(0) < nu_ref[0])
    def _():
        @pl.when(_expert_changed(be_ref))
        def _():
            wg_bf[...] = wg_ref[0].astype(BF16)
            wu_bf[...] = wu_ref[0].astype(BF16)

        u = xs_ref[...]
        half = u.shape[1]
        lo = pltpu.bitcast(u << 16, F32).astype(BF16)
        hi = pltpu.bitcast(u & jnp.uint32(0xFFFF0000), F32).astype(BF16)

        def proj(w_bf):
            return (jnp.dot(lo, w_bf[:half, :], preferred_element_type=F32)
                    + jnp.dot(hi, w_bf[half:, :], preferred_element_type=F32))

        gate = proj(wg_bf)
        o_ref[...] = (gate * _sigmoid(gate) * proj(wu_bf)).astype(o_ref.dtype)

    @pl.when(pl.program_id(0) >= nu_ref[0])
    def _():
        o_ref[...] = jnp.zeros_like(o_ref)


def _expert_down_kernel(be_ref, nu_ref, h_ref, wd_ref, o_ref, wd_bf):
    @pl.when(pl.program_id(0) < nu_ref[0])
    def _():
        @pl.when(_expert_changed(be_ref))
        def _():
            wd_bf[...] = wd_ref[0].astype(BF16)

        o_ref[...] = jnp.dot(h_ref[...], wd_bf[...], preferred_element_type=F32)

    @pl.when(pl.program_id(0) >= nu_ref[0])
    def _():
        o_ref[...] = jnp.zeros_like(o_ref)


def _experts(block_e, n_used, xs, w_e_gate, w_e_up, w_e_down, layer):
    n_rows, half = xs.shape
    d = 2 * half
    hid = w_e_gate.shape[2]
    n_blocks = n_rows // MOE_BLOCK
    blk = lambda i, be, nu: jnp.minimum(i, nu[0] - 1)
    expert = lambda i, be, nu: (layer * N_EXPERTS + be[blk(i, be, nu)], 0, 0)
    hidden = pl.pallas_call(
        _expert_hidden_kernel,
        grid_spec=pltpu.PrefetchScalarGridSpec(
            num_scalar_prefetch=2,
            grid=(n_blocks,),
            in_specs=[pl.BlockSpec((MOE_BLOCK, half), lambda i, be, nu: (blk(i, be, nu), 0)),
                      pl.BlockSpec((1, d, hid), expert),
                      pl.BlockSpec((1, d, hid), expert)],
            out_specs=pl.BlockSpec((MOE_BLOCK, hid), lambda i, be, nu: (i, 0)),
            scratch_shapes=[pltpu.VMEM((d, hid), BF16), pltpu.VMEM((d, hid), BF16)],
        ),
        out_shape=jax.ShapeDtypeStruct((n_rows, hid), BF16),
        compiler_params=_cparams("arbitrary"),
    )(block_e, n_used, xs, w_e_gate, w_e_up)
    return pl.pallas_call(
        _expert_down_kernel,
        grid_spec=pltpu.PrefetchScalarGridSpec(
            num_scalar_prefetch=2,
            grid=(n_blocks,),
            in_specs=[pl.BlockSpec((MOE_BLOCK, hid), lambda i, be, nu: (blk(i, be, nu), 0)),
                      pl.BlockSpec((1, hid, d), expert)],
            out_specs=pl.BlockSpec((MOE_BLOCK, d), lambda i, be, nu: (i, 0)),
            scratch_shapes=[pltpu.VMEM((hid, d), BF16)],
        ),
        out_shape=jax.ShapeDtypeStruct((n_rows, d), F32),
        compiler_params=_cparams("arbitrary"),
    )(block_e, n_used, hidden, w_e_down)


def _combine_kernel(dest_ref, x_ref, info_ref, g_ref, y_ref, *rest, tokens_per_step, final):
    if final:
        o_ref, buf, sem = rest
    else:
        x2_ref, o_ref, buf, sem = rest
    step = pl.program_id(0)

    def gather(tile, slot):
        def issue(j, carry):
            tok = tile * tokens_per_step + j
            _row_copy(y_ref, dest_ref[2 * tok], buf.at[slot, 0], j, 1, sem.at[slot]).start()
            _row_copy(y_ref, dest_ref[2 * tok + 1], buf.at[slot, 1], j, 1, sem.at[slot]).start()
            return carry

        lax.fori_loop(0, tokens_per_step, issue, 0)

    @pl.when(step == 0)
    def _():
        gather(0, 0)

    @pl.when(step + 1 < pl.num_programs(0))
    def _():
        gather(step + 1, (step + 1) % 2)

    slot = step % 2
    _row_copy(y_ref, 0, buf.at[slot, 0], 0, tokens_per_step, sem.at[slot]).wait()
    _row_copy(y_ref, 0, buf.at[slot, 1], 0, tokens_per_step, sem.at[slot]).wait()
    info = info_ref[...]
    x2 = x_ref[...] + (buf[slot, 0] * info[:, 4:5] + buf[slot, 1] * info[:, 5:6])
    if not final:
        x2_ref[...] = x2
    o_ref[...] = _rms(x2, g_ref[...]).astype(o_ref.dtype)


def _combine(dest, x, info, g_next, y_rows, final):
    t, d = x.shape
    tc = min(t, 256)
    tile = pl.BlockSpec((tc, d), lambda i, dr: (i, 0))
    out_specs = [tile] if final else [tile, tile]
    out_shape = ([jax.ShapeDtypeStruct((t, d), F32)] if final
                 else [jax.ShapeDtypeStruct((t, d), F32), jax.ShapeDtypeStruct((t, d), BF16)])
    return pl.pallas_call(
        functools.partial(_combine_kernel, tokens_per_step=tc, final=final),
        grid_spec=pltpu.PrefetchScalarGridSpec(
            num_scalar_prefetch=1,
            grid=(t // tc,),
            in_specs=[tile,
                      pl.BlockSpec((tc, LANES), lambda i, dr: (i, 0)),
                      pl.BlockSpec((1, d), lambda i, dr: (0, 0)),
                      pl.BlockSpec(memory_space=pl.ANY)],
            out_specs=out_specs,
            scratch_shapes=[pltpu.VMEM((2, 2, tc, d), F32), pltpu.SemaphoreType.DMA((2,))],
        ),
        out_shape=out_shape,
        compiler_params=_cparams("arbitrary"),
    )(dest, x, info, g_next.reshape(1, d), y_rows)


def _pad_cols(w, width):
    return jnp.pad(w, ((0, 0), (0, width - w.shape[1])))


def _stage_w_in(w_in, d):
    dsa_heads = d // (4 * HEAD_DIM)
    hpg = d // 1024
    dil_heads = len(DIL_PATTERNS) * hpg
    a_w, b_w = dsa_heads * HEAD_DIM, dil_heads * HEAD_DIM
    pool_w = d - a_w - b_w
    sizes = (a_w, HEAD_DIM, HEAD_DIM, IDX_HEADS * IDX_DIM, IDX_DIM, IDX_HEADS, b_w, b_w, b_w, pool_w)
    offs = [0]
    for s in sizes:
        offs.append(offs[-1] + s)
    cut = offs[6]
    shift = -cut % MXU_WIDTH
    w_al = jnp.concatenate([w_in[:, :cut].astype(BF16), jnp.zeros((w_in.shape[0], shift), BF16),
                            w_in[:, cut:].astype(BF16)], axis=1)
    start = [o + (shift if n >= 6 else 0) for n, o in enumerate(offs)]
    qa, ka, va, qi, ki, wi, qb, kb, vb, c, end = start
    assert all(o % MXU_WIDTH == 0 for o in (qa, ka, qi, ki, qb, kb, vb, c, end))
    assert va == ka + HEAD_DIM and qi == ka + MXU_WIDTH and qb == ki + MXU_WIDTH and wi == ki + IDX_DIM
    tiles = lambda lo, hi: list(range(lo // MXU_WIDTH, hi // MXU_WIDTH))
    t_kav, t_kiw = tiles(ka, qi), tiles(ki, qb)
    calls = dict(rope=tiles(qa, ka) + t_kav + tiles(qb, vb), idx=tiles(qi, ki) + t_kiw,
                 val=t_kav + tiles(vb, c), pool=tiles(c, end) + t_kiw)
    lay = dict(dsa_heads=dsa_heads, dil_heads_per_group=hpg,
               ka=a_w, qb=a_w + MXU_WIDTH, kb=a_w + MXU_WIDTH + b_w,
               ki=IDX_HEADS * IDX_DIM,
               va=HEAD_DIM, vb=MXU_WIDTH,
               wi=pool_w, wi_lane=IDX_DIM)
    return w_al, calls, lay


def kernel(x, positions, norm1_g, w_in, w_gate, b_gate, w_br_a, w_br_b, w_br_c, w_pool, pool_scale,
           w_out, norm2_g, w_rg, b_rg, w_re, b_re, w_e_gate, w_e_up, w_e_down, final_g):
    bsz, seq, d = x.shape
    depth = w_in.shape[0]
    t = bsz * seq
    n_blocks = -(-(2 * t) // MOE_BLOCK) + N_EXPERTS
    n_rows = n_blocks * MOE_BLOCK

    c128, s128, c64, s64 = _rope_tables(positions)
    xt = x.reshape(t, d)
    h = _rms_norm(xt, norm1_g[0], BF16)
    out = None
    for l in range(depth):
        w_al, tiles, lay = _stage_w_in(w_in[l], d)
        zr = _project(h, w_al, tiles["rope"], BF16, rope=(HEAD_DIM, c128, s128))
        zi = _project(h, w_al, tiles["idx"], BF16, rope=(IDX_DIM, c64, s64))
        zv = _project(h, w_al, tiles["val"], BF16)
        zc = _project(h, w_al, tiles["pool"], F32)

        a_out = _dsa(zr, zi, zv, zc, bsz, seq, lay)
        b_out = _dilated(zr, zv, bsz, seq, lay)
        c_out = _pool(zc, w_pool[l].astype(BF16), pool_scale[l], bsz, seq)

        merged = _merge(h, w_gate[l].astype(BF16), b_gate[l], a_out, b_out, c_out,
                        w_br_a[l].astype(BF16), w_br_b[l].astype(BF16), w_br_c[l].astype(BF16))
        x1 = _outproj(merged, w_out[l].astype(BF16), xt)

        w_router = _pad_cols(jnp.concatenate([w_re[l], w_rg[l]], axis=1), LANES).astype(BF16)
        b_router = _pad_cols(jnp.concatenate([b_re[l], b_rg[l]])[None, :], LANES)
        hp, info, counts = _router(x1, norm2_g[l], w_router, b_router)
        dest2, meta = _plan(counts, info, n_blocks)
        dest = dest2[:, :2].reshape(2 * t)
        block_e = meta[:n_blocks, 0]
        n_used = meta[:1, 1]
        xs = _dispatch(dest, hp, n_rows)
        stack = lambda w: w.reshape((depth * N_EXPERTS,) + w.shape[2:])
        y_rows = _experts(block_e, n_used, xs, stack(w_e_gate), stack(w_e_up), stack(w_e_down), l)
        final = l == depth - 1
        g_next = final_g if final else norm1_g[l + 1]
        res = _combine(dest, x1, info, g_next, y_rows, final)
        if final:
            out = res[0]
        else:
            xt, h = res
    return out.reshape(bsz, seq, d)
```

```python
import functools

import jax
import jax.numpy as jnp
from jax import lax
from jax.experimental import pallas as pl
from jax.experimental.pallas import tpu as pltpu

F32 = jnp.float32
BF16 = jnp.bfloat16
I32 = jnp.int32
U32 = jnp.uint32

LANES = 128
MXU_WIDTH = 256
VMEM_LIMIT = 56 * 1024 * 1024

HEAD_DIM = 128
ROPE_THETA = 500000.0
ROPE_FRACTION = 4
NORM_EPS = 1e-6
IDX_HEADS = 16
IDX_DIM = 64
DSA_TOPK = 256
Q_BLOCK = 128
DIL_PATTERNS = ((128, 1), (512, 4), (2048, 16))
POOL_WINDOWS = (2, 4, 8, 16)
N_EXPERT_GROUPS = 4
EXPERTS_PER_GROUP = 8
N_EXPERTS = N_EXPERT_GROUPS * EXPERTS_PER_GROUP
MOE_BLOCK = 512
NEG_INF = float("-inf")


def _cparams(*sem):
    return pltpu.CompilerParams(dimension_semantics=sem, vmem_limit_bytes=VMEM_LIMIT)


def _rope_table_kernel(pos_ref, f128_ref, s128_ref, f64_ref, s64_ref, c128, sn128, c64, sn64):
    p = pos_ref[...].astype(F32)
    a = p * f128_ref[...]
    c128[...] = jnp.cos(a)
    sn128[...] = jnp.sin(a) * s128_ref[...]
    a = p * f64_ref[...]
    c64[...] = jnp.cos(a)
    sn64[...] = jnp.sin(a) * s64_ref[...]


def _rope_lane_patterns():
    lane = jnp.arange(LANES)

    def pattern(width):
        rd = width // ROPE_FRACTION
        half = rd // 2
        inv = ROPE_THETA ** (-jnp.arange(half, dtype=F32) * (2.0 / rd))
        l = lane % width
        freq = jnp.where(l < rd, inv[l % half], 0.0).astype(F32)
        sign = jnp.where(l < half, -1.0, jnp.where(l < rd, 1.0, 0.0)).astype(F32)
        return freq[None, :], sign[None, :]

    return pattern(HEAD_DIM) + pattern(IDX_DIM)


def _rope_tables(positions):
    t = positions.size
    tm = min(t, 1024)
    f128, s128, f64, s64 = _rope_lane_patterns()
    vec = pl.BlockSpec((1, LANES), lambda i: (0, 0))
    tab = pl.BlockSpec((tm, LANES), lambda i: (i, 0))
    return pl.pallas_call(
        _rope_table_kernel,
        grid=(t // tm,),
        in_specs=[pl.BlockSpec((tm, 1), lambda i: (i, 0)), vec, vec, vec, vec],
        out_specs=[tab] * 4,
        out_shape=[jax.ShapeDtypeStruct((t, LANES), F32)] * 4,
        compiler_params=_cparams("arbitrary"),
    )(positions.reshape(t, 1), f128, s128, f64, s64)


def _rms(x, g):
    ms = jnp.mean(x * x, axis=-1, keepdims=True)
    return x * lax.rsqrt(ms + NORM_EPS) * g


def _norm_kernel(x_ref, g_ref, o_ref):
    o_ref[...] = _rms(x_ref[...], g_ref[...]).astype(o_ref.dtype)


def _rms_norm(x, g, out_dtype):
    t, d = x.shape
    tm = min(t, 512)
    return pl.pallas_call(
        _norm_kernel,
        grid=(t // tm,),
        in_specs=[pl.BlockSpec((tm, d), lambda i: (i, 0)), pl.BlockSpec((1, d), lambda i: (0, 0))],
        out_specs=pl.BlockSpec((tm, d), lambda i: (i, 0)),
        out_shape=jax.ShapeDtypeStruct((t, d), out_dtype),
        compiler_params=_cparams("arbitrary"),
    )(x, g.reshape(1, d))


PROJ_SUB_ROWS = 512


def _proj_kernel(tiles_ref, x_ref, w_ref, *rest, rope_width):
    del tiles_ref
    o_ref = rest[-1]
    sub = min(PROJ_SUB_ROWS, x_ref.shape[0])
    for r in range(x_ref.shape[0] // sub):
        rows = pl.ds(r * sub, sub)
        acc = jnp.dot(x_ref[rows, :], w_ref[...], preferred_element_type=F32)
        if rope_width is None:
            o_ref[rows, :] = acc.astype(o_ref.dtype)
            continue
        c_ref, s_ref = rest[:2]
        half = rope_width // ROPE_FRACTION // 2
        c = c_ref[rows, :]
        s = s_ref[rows, :]
        lane = lax.broadcasted_iota(I32, c.shape, 1)
        first_half = (lane % rope_width) < half
        for g in range(acc.shape[1] // LANES):
            z = acc[:, g * LANES:(g + 1) * LANES]
            partner = jnp.where(first_half, pltpu.roll(z, LANES - half, 1), pltpu.roll(z, half, 1))
            o_ref[rows, g * LANES:(g + 1) * LANES] = (z * c + partner * s).astype(o_ref.dtype)


def _project(h, w, tiles, out_dtype, rope=None):
    t, k = h.shape
    tn = MXU_WIDTH
    tm = min(t, 2048)
    in_specs = [pl.BlockSpec((tm, k), lambda i, j, tl: (i, 0)),
                pl.BlockSpec((k, tn), lambda i, j, tl: (0, tl[j]))]
    args = [h, w]
    if rope is not None:
        tab = pl.BlockSpec((tm, LANES), lambda i, j, tl: (i, 0))
        in_specs += [tab, tab]
        args += [rope[1], rope[2]]
    return pl.pallas_call(
        functools.partial(_proj_kernel, rope_width=None if rope is None else rope[0]),
        grid_spec=pltpu.PrefetchScalarGridSpec(
            num_scalar_prefetch=1,
            grid=(t // tm, len(tiles)),
            in_specs=in_specs,
            out_specs=pl.BlockSpec((tm, tn), lambda i, j, tl: (i, j)),
        ),
        out_shape=jax.ShapeDtypeStruct((t, len(tiles) * tn), out_dtype),
        compiler_params=_cparams("arbitrary", "arbitrary"),
    )(jnp.asarray(tiles, I32), *args)


KEY_CHUNK = 256
SUBLANES = 8


def _sublane_allreduce(x, op):
    for shift in (4, 2, 1):
        x = op(x, pltpu.roll(x, shift, 0))
    return x


def _tree_reduce(x, op):
    while x.shape[0] > 1:
        half = x.shape[0] // 2
        x = op(x[:half], x[half:])
    return x[0]


def _vregs(x):
    return x.reshape(x.shape[0] // SUBLANES, SUBLANES, LANES)


def _dsa_kernel(qa_ref, ka_ref, va_ref, qi_ref, ki_ref, wi_ref, o_ref,
                key_ref, vat_ref, kil_ref, kih_ref, acc_ref, m_ref, l_ref, *, topk, n_heads, wi_lane):
    nq = o_ref.shape[0]
    ck = KEY_CHUNK
    i = pl.program_id(1)
    t0 = i * nq
    n_chunks = (t0 + nq + ck - 1) // ck
    nt = (((1,), (1,)), ((), ()))

    @pl.when(i == 0)
    def _():
        for c in range(vat_ref.shape[0]):
            vat_ref[c] = va_ref[c * ck:(c + 1) * ck, :].astype(F32).T.astype(BF16)
        ki = ki_ref[...].astype(F32)
        ki = jnp.where(lax.broadcasted_iota(I32, ki.shape, 1) < IDX_DIM, ki, 0.0)
        kil_ref[...] = ki.astype(BF16)
        kih_ref[...] = pltpu.roll(ki, IDX_DIM, 1).astype(BF16)

    def causal_mask(c):
        shape = (ck // SUBLANES, SUBLANES, nq)
        spos = (c * ck + lax.broadcasted_iota(I32, shape, 0) * SUBLANES
                + lax.broadcasted_iota(I32, shape, 1))
        tpos = t0 + lax.broadcasted_iota(I32, shape, 2)
        return spos <= tpos

    wt = (wi_ref[...] * ((IDX_DIM ** -0.5) * (IDX_HEADS ** -0.5))).T
    w8 = [jnp.broadcast_to(wt[wi_lane + h:wi_lane + h + 1, :], (SUBLANES, LANES)) for h in range(IDX_HEADS)]
    q_stack = [jnp.concatenate([qi_ref[:, (2 * m) * LANES:(2 * m + 1) * LANES],
                                qi_ref[:, (2 * m + 1) * LANES:(2 * m + 2) * LANES]], axis=0)
               for m in range(IDX_HEADS // 4)]

    def score_chunk(c, carry):
        r0 = pl.multiple_of(c * ck, ck)
        kil = kil_ref[pl.ds(r0, ck), :]
        kih = kih_ref[pl.ds(r0, ck), :]
        acc = jnp.zeros((ck // SUBLANES, SUBLANES, LANES), F32)
        for m in range(IDX_HEADS // 4):
            d_even = lax.dot_general(kil, q_stack[m], nt, preferred_element_type=F32)
            d_odd = lax.dot_general(kih, q_stack[m], nt, preferred_element_type=F32)
            for dots, h in ((d_even[:, :nq], 4 * m), (d_odd[:, :nq], 4 * m + 1),
                            (d_even[:, nq:], 4 * m + 2), (d_odd[:, nq:], 4 * m + 3)):
                acc = acc + jnp.maximum(_vregs(dots), 0.0) * w8[h][None]
        score = jnp.where(causal_mask(c), acc, NEG_INF).reshape(ck, nq)
        bits = pltpu.bitcast(score, I32)
        key_ref[c] = bits ^ ((bits >> 31) & jnp.int32(0x7FFFFFFF))
        return carry

    lax.fori_loop(0, n_chunks, score_chunk, 0)

    def count(compare, cand):
        def body(c, cnt):
            return cnt + _tree_reduce(jnp.where(compare(_vregs(key_ref[c]), cand[None]), 1.0, 0.0), jnp.add)
        cnt = lax.fori_loop(0, n_chunks, body, jnp.zeros((SUBLANES, LANES), F32))
        return _sublane_allreduce(cnt, jnp.add)

    count_ge = functools.partial(count, jnp.greater_equal)
    int_min = jnp.int32(-2 ** 31)
    thr = jnp.where(count_ge(jnp.zeros((SUBLANES, LANES), I32)) >= topk, jnp.int32(0), int_min)

    def refine(it, thr):
        cand = thr | (jnp.int32(1) << (30 - it))
        return jnp.where(count_ge(cand) >= topk, cand, thr)

    thr = lax.fori_loop(0, 31, refine, thr)
    tie_slots = topk - count(jnp.greater, thr)
    r = lax.broadcasted_iota(I32, (ck, ck), 0)
    cidx = lax.broadcasted_iota(I32, (ck, ck), 1)
    earlier = jnp.where(cidx < r, 1.0, 0.0).astype(BF16)

    qt = jnp.concatenate([qa_ref[:, h * HEAD_DIM:(h + 1) * HEAD_DIM].astype(F32).T for h in range(n_heads)],
                         axis=1).astype(BF16)
    m_ref[...] = jnp.full(m_ref.shape, NEG_INF, F32)
    l_ref[...] = jnp.zeros(l_ref.shape, F32)
    acc_ref[...] = jnp.zeros(acc_ref.shape, F32)

    def attend(c, ties_seen):
        r0 = pl.multiple_of(c * ck, ck)
        key = _vregs(key_ref[c])
        tie = jnp.where(key == thr[None], 1.0, 0.0)
        ties_before = _vregs(jnp.dot(earlier, tie.reshape(ck, nq).astype(BF16),
                                     preferred_element_type=F32)) + ties_seen[None]
        tie_bias = jnp.where(ties_before < tie_slots[None], tie - 1.0, NEG_INF)
        bias = jnp.where(key > thr[None], 0.0, jnp.where(tie > 0.0, tie_bias, NEG_INF))
        bias = jnp.where(causal_mask(c), bias, NEG_INF)
        ties_seen = ties_seen + _sublane_allreduce(_tree_reduce(tie, jnp.add), jnp.add)
        s_all = jnp.dot(ka_ref[pl.ds(r0, ck), :], qt, preferred_element_type=F32) * (HEAD_DIM ** -0.5)
        ps, alphas = [], []
        for h in range(n_heads):
            s = _vregs(s_all[:, h * nq:(h + 1) * nq]) + bias
            m_old = m_ref[h]
            m_new = jnp.maximum(m_old, _sublane_allreduce(_tree_reduce(s, jnp.maximum), jnp.maximum))
            m_safe = jnp.where(m_new == NEG_INF, 0.0, m_new)
            alphas.append(jnp.exp(m_old - m_safe))
            p = jnp.exp(s - m_safe[None])
            l_ref[h] = alphas[h] * l_ref[h] + _tree_reduce(p, jnp.add)
            ps.append(p.reshape(ck, nq).astype(BF16))
            m_ref[h] = m_new
        pv = jnp.dot(vat_ref[c], jnp.concatenate(ps, axis=1), preferred_element_type=F32)
        for h in range(n_heads):
            acc_ref[h] = (_vregs(acc_ref[h]) * alphas[h][None]).reshape(HEAD_DIM, nq) + pv[:, h * nq:(h + 1) * nq]
        return ties_seen

    lax.fori_loop(0, n_chunks, attend, jnp.zeros((SUBLANES, LANES), F32))
    for h in range(n_heads):
        l = _sublane_allreduce(l_ref[h], jnp.add)
        o = (_vregs(acc_ref[h]) / l[None]).reshape(HEAD_DIM, nq)
        o_ref[:, h * HEAD_DIM:(h + 1) * HEAD_DIM] = o.T.astype(o_ref.dtype)


def _dsa(zr, zi, zv, zc, bsz, seq, lay):
    t = bsz * seq
    nqb = seq // Q_BLOCK
    n_heads = lay["dsa_heads"]
    a_w = n_heads * HEAD_DIM
    qw = IDX_HEADS * IDX_DIM
    nck = seq // KEY_CHUNK
    row = lambda b, i: b * nqb + i
    in_specs = [
        pl.BlockSpec((Q_BLOCK, a_w), lambda b, i: (row(b, i), 0)),
        pl.BlockSpec((seq, HEAD_DIM), lambda b, i: (b, lay["ka"] // HEAD_DIM)),
        pl.BlockSpec((seq, HEAD_DIM), lambda b, i: (b, lay["va"] // HEAD_DIM)),
        pl.BlockSpec((Q_BLOCK, qw), lambda b, i: (row(b, i), 0)),
        pl.BlockSpec((seq, LANES), lambda b, i: (b, lay["ki"] // LANES)),
        pl.BlockSpec((Q_BLOCK, LANES), lambda b, i: (row(b, i), lay["wi"] // LANES)),
    ]
    return pl.pallas_call(
        functools.partial(_dsa_kernel, topk=min(DSA_TOPK, seq // 4), n_heads=n_heads, wi_lane=lay["wi_lane"]),
        grid=(bsz, nqb),
        in_specs=in_specs,
        out_specs=pl.BlockSpec((Q_BLOCK, a_w), lambda b, i: (row(b, i), 0)),
        out_shape=jax.ShapeDtypeStruct((t, a_w), BF16),
        scratch_shapes=[pltpu.VMEM((nck, KEY_CHUNK, Q_BLOCK), I32),
                        pltpu.VMEM((nck, HEAD_DIM, KEY_CHUNK), BF16),
                        pltpu.VMEM((seq, LANES), BF16),
                        pltpu.VMEM((seq, LANES), BF16),
                        pltpu.VMEM((n_heads, HEAD_DIM, Q_BLOCK), F32),
                        pltpu.VMEM((n_heads, SUBLANES, Q_BLOCK), F32),
                        pltpu.VMEM((n_heads, SUBLANES, Q_BLOCK), F32)],
        compiler_params=_cparams("arbitrary", "arbitrary"),
    )(zr, zr, zv, zi, zi, zc)


MIX_ROWS = 256


def _band_attention(q, k_blocks, v_blocks):
    bs = q.shape[0]
    nt = (((1,), (1,)), ((), ()))
    k = k_blocks[0] if len(k_blocks) == 1 else jnp.concatenate(k_blocks, axis=0)
    v = v_blocks[0] if len(v_blocks) == 1 else jnp.concatenate(v_blocks, axis=0)
    s = lax.dot_general(q, k, nt, preferred_element_type=F32) * (HEAD_DIM ** -0.5)
    a = lax.broadcasted_iota(I32, s.shape, 0)
    c = lax.broadcasted_iota(I32, s.shape, 1)
    if len(k_blocks) == 1:
        valid = c <= a
    else:
        valid = (a + bs - c).astype(U32) <= jnp.uint32(bs)
    s = jnp.where(valid, s, NEG_INF)
    m = jnp.max(s, axis=1, keepdims=True)
    p = jnp.exp(s - m)
    l = jnp.sum(p, axis=1, keepdims=True)
    o = jnp.dot(p.astype(BF16), v, preferred_element_type=F32) / l
    return o, m + jnp.log(l)


def _dil_kernel(*refs, seq):
    ng = len(DIL_PATTERNS)
    q_refs, k_refs, v_refs = refs[:ng], refs[ng:2 * ng], refs[2 * ng:3 * ng]
    o_ref, qf, kf, vf, of, lf = refs[3 * ng:]
    for g, (window, dil) in enumerate(DIL_PATTERNS):
        ls = seq // dil
        bs = min(ls, Q_BLOCK)
        assert ls % bs == 0 and window // dil == Q_BLOCK
        if dil > 1:
            qf[...] = q_refs[g][...].astype(F32)
            kf[...] = k_refs[g][...].astype(F32)
            vf[...] = v_refs[g][...].astype(F32)
        for x in range(dil):
            def rows(n):
                return pl.ds(x + dil * bs * n, bs, stride=dil) if dil > 1 else pl.ds(bs * n, bs)

            def load(src_bf16, src_f32, n):
                return src_bf16[rows(n), :] if dil == 1 else src_f32[rows(n), :].astype(BF16)

            k_prev = v_prev = None
            for n in range(ls // bs):
                q = load(q_refs[g], qf, n)
                k_cur = load(k_refs[g], kf, n)
                v_cur = load(v_refs[g], vf, n)
                if n == 0:
                    o, lse = _band_attention(q, [k_cur], [v_cur])
                else:
                    o, lse = _band_attention(q, [k_prev, k_cur], [v_prev, v_cur])
                of[g, rows(n), :] = o
                lf[g, rows(n), :] = jnp.broadcast_to(lse, o.shape)
                k_prev, v_prev = k_cur, v_cur
    mr = min(MIX_ROWS, seq)
    for r in range(seq // mr):
        sl = pl.ds(r * mr, mr)
        lses = [lf[g, sl, :] for g in range(ng)]
        mx = functools.reduce(jnp.maximum, lses)
        es = [jnp.exp(x - mx) for x in lses]
        tot = functools.reduce(lambda u, w: u + w, es)
        acc = functools.reduce(lambda u, w: u + w, [(e / tot) * of[g, sl, :] for g, e in enumerate(es)])
        o_ref[sl, :] = acc.astype(o_ref.dtype)


def _dilated(zr, zv, bsz, seq, lay):
    t = bsz * seq
    hpg = lay["dil_heads_per_group"]
    ng = len(DIL_PATTERNS)

    def head_spec(base, g):
        return pl.BlockSpec((seq, HEAD_DIM), lambda b, s: (b, base // HEAD_DIM + g * hpg + s))

    in_specs = ([head_spec(lay["qb"], g) for g in range(ng)]
                + [head_spec(lay["kb"], g) for g in range(ng)]
                + [head_spec(lay["vb"], g) for g in range(ng)])
    return pl.pallas_call(
        functools.partial(_dil_kernel, seq=seq),
        grid=(bsz, hpg),
        in_specs=in_specs,
        out_specs=pl.BlockSpec((seq, HEAD_DIM), lambda b, s: (b, s)),
        out_shape=jax.ShapeDtypeStruct((t, hpg * HEAD_DIM), BF16),
        scratch_shapes=[pltpu.VMEM((seq, HEAD_DIM), F32)] * 3
                       + [pltpu.VMEM((ng, seq, HEAD_DIM), F32)] * 2,
        compiler_params=_cparams("arbitrary", "arbitrary"),
    )(*([zr] * (2 * ng) + [zv] * ng))


def _pool_kernel(c_ref, wp_ref, ps_ref, o_ref):
    x = c_ref[...]
    row = lax.broadcasted_iota(I32, x.shape, 0)

    def run(window):
        acc = x
        k = 1
        while k < window:
            acc = acc + jnp.where(row >= k, pltpu.roll(acc, k, 0), 0.0)
            k *= 2
        denom = jnp.minimum(row[:, :1] + 1, window).astype(F32)
        y = (acc / denom - x).astype(BF16)
        o = jnp.dot(y, wp_ref[0], preferred_element_type=F32) * ps_ref[...]
        o_ref[...] = o.astype(o_ref.dtype)

    for gi, window in enumerate(POOL_WINDOWS):
        pl.when(pl.program_id(1) == gi)(functools.partial(run, window))


def _pool(zc, w_pool, pool_scale, bsz, seq):
    t = bsz * seq
    ng, gw = w_pool.shape[0], w_pool.shape[1]
    return pl.pallas_call(
        _pool_kernel,
        grid=(bsz, ng),
        in_specs=[pl.BlockSpec((seq, gw), lambda b, g: (b, g)),
                  pl.BlockSpec((1, gw, gw), lambda b, g: (g, 0, 0)),
                  pl.BlockSpec((1, gw), lambda b, g: (0, g))],
        out_specs=pl.BlockSpec((seq, gw), lambda b, g: (b, g)),
        out_shape=jax.ShapeDtypeStruct((t, ng * gw), BF16),
        compiler_params=_cparams("arbitrary", "arbitrary"),
    )(zc, w_pool, pool_scale.reshape(1, ng * gw))


def _sigmoid(x):
    return 1.0 / (1.0 + jnp.exp(-x))


def _merge_kernel(h_ref, wg0, wg1, wg2, bg0, bg1, bg2, a_ref, b_ref, c_ref, wa, wb, wc, o_ref):
    h = h_ref[...]
    acc = None
    for wg, bg, br, wbr in ((wg0, bg0, a_ref, wa), (wg1, bg1, b_ref, wb), (wg2, bg2, c_ref, wc)):
        gate = _sigmoid(jnp.dot(h, wg[...], preferred_element_type=F32) + bg[...])
        term = gate * jnp.dot(br[...], wbr[...], preferred_element_type=F32)
        acc = term if acc is None else acc + term
    o_ref[...] = acc.astype(o_ref.dtype)


def _merge(h, w_gate, b_gate, a_out, b_out, c_out, w_a, w_b, w_c):
    t, d = h.shape
    tm = min(t, 1024)
    tn = 256
    nd = d // tn
    rows = lambda width: pl.BlockSpec((tm, width), lambda i, j: (i, 0))
    cols = lambda k, off: pl.BlockSpec((k, tn), lambda i, j: (0, off * nd + j))
    in_specs = ([rows(d)] + [cols(d, n) for n in range(3)] + [cols(1, n) for n in range(3)]
                + [rows(a_out.shape[1]), rows(b_out.shape[1]), rows(c_out.shape[1])]
                + [cols(a_out.shape[1], 0), cols(b_out.shape[1], 0), cols(c_out.shape[1], 0)])
    bg = b_gate.reshape(1, 3 * d)
    return pl.pallas_call(
        _merge_kernel,
        grid=(t // tm, nd),
        in_specs=in_specs,
        out_specs=pl.BlockSpec((tm, tn), lambda i, j: (i, j)),
        out_shape=jax.ShapeDtypeStruct((t, d), BF16),
        compiler_params=_cparams("arbitrary", "arbitrary"),
    )(h, w_gate, w_gate, w_gate, bg, bg, bg, a_out, b_out, c_out, w_a, w_b, w_c)


def _outproj_kernel(m_ref, w_ref, x_ref, o_ref):
    o_ref[...] = x_ref[...] + jnp.dot(m_ref[...], w_ref[...], preferred_element_type=F32)


def _outproj(merged, w_out, x):
    t, d = x.shape
    tm = min(t, 1024)
    tn = 512
    return pl.pallas_call(
        _outproj_kernel,
        grid=(t // tm, d // tn),
        in_specs=[pl.BlockSpec((tm, d), lambda i, j: (i, 0)),
                  pl.BlockSpec((d, tn), lambda i, j: (0, j)),
                  pl.BlockSpec((tm, tn), lambda i, j: (i, j))],
        out_specs=pl.BlockSpec((tm, tn), lambda i, j: (i, j)),
        out_shape=jax.ShapeDtypeStruct((t, d), F32),
        compiler_params=_cparams("arbitrary", "arbitrary"),
    )(merged, w_out, x)


GROUP_LANE0 = N_EXPERTS


def _lane_min_index(mask, lane):
    return jnp.min(jnp.where(mask, lane, LANES), axis=1, keepdims=True)


def _router_kernel(x_ref, g_ref, wr_ref, br_ref, hp_ref, info_ref, cnt_ref, carry_ref):
    @pl.when(pl.program_id(0) == 0)
    def _():
        carry_ref[...] = jnp.zeros_like(carry_ref)

    hb = _rms(x_ref[...], g_ref[...]).astype(BF16)
    tm, d = hb.shape
    lo = pltpu.bitcast(hb[:, :d // 2].astype(F32), U32)
    hi = pltpu.bitcast(hb[:, d // 2:].astype(F32), U32)
    hp_ref[...] = hi | (lo >> 16)

    logits = jnp.dot(hb, wr_ref[...], preferred_element_type=F32) + br_ref[...]
    lane = lax.broadcasted_iota(I32, logits.shape, 1)
    is_group = (lane >= GROUP_LANE0) & (lane < GROUP_LANE0 + N_EXPERT_GROUPS)
    lg = jnp.where(is_group, logits, NEG_INF)
    gmax = jnp.max(lg, axis=1, keepdims=True)
    gsel = _lane_min_index(lg == gmax, lane) - GROUP_LANE0
    pg_sel = 1.0 / jnp.sum(jnp.where(is_group, jnp.exp(lg - gmax), 0.0), axis=1, keepdims=True)
    in_group = (lane < N_EXPERTS) & ((lane // EXPERTS_PER_GROUP) == gsel)
    le = jnp.where(in_group, logits, NEG_INF)
    ee = jnp.where(in_group, jnp.exp(le - jnp.max(le, axis=1, keepdims=True)), 0.0)
    pe = jnp.where(in_group, ee / jnp.sum(ee, axis=1, keepdims=True), -1.0)
    v0 = jnp.max(pe, axis=1, keepdims=True)
    e0 = _lane_min_index(pe == v0, lane)
    pe1 = jnp.where(lane == e0, -1.0, pe)
    v1 = jnp.max(pe1, axis=1, keepdims=True)
    e1 = _lane_min_index(pe1 == v1, lane)
    w0 = pg_sel * v0 / (v0 + v1)
    w1 = pg_sel * v1 / (v0 + v1)
    onehot = jnp.where((lane == e0) | (lane == e1), 1.0, 0.0)
    r = lax.broadcasted_iota(I32, (tm, tm), 0)
    c = lax.broadcasted_iota(I32, (tm, tm), 1)
    before = jnp.where(c < r, 1.0, 0.0).astype(BF16)
    prior = jnp.dot(before, onehot.astype(BF16), preferred_element_type=F32) + carry_ref[...]
    r0 = jnp.sum(jnp.where(lane == e0, prior, 0.0), axis=1, keepdims=True)
    r1 = jnp.sum(jnp.where(lane == e1, prior, 0.0), axis=1, keepdims=True)
    carry_ref[...] = carry_ref[...] + jnp.sum(onehot, axis=0, keepdims=True)
    cnt_ref[...] = carry_ref[...]
    info = jnp.zeros_like(logits)
    for k, val in enumerate((e0.astype(F32), e1.astype(F32), r0, r1, w0, w1)):
        info = jnp.where(lane == k, val, info)
    info_ref[...] = info


def _router(x, g, w_router, b_router):
    t, d = x.shape
    tm = min(t, 256)
    return pl.pallas_call(
        _router_kernel,
        grid=(t // tm,),
        in_specs=[pl.BlockSpec((tm, d), lambda i: (i, 0)),
                  pl.BlockSpec((1, d), lambda i: (0, 0)),
                  pl.BlockSpec((d, LANES), lambda i: (0, 0)),
                  pl.BlockSpec((1, LANES), lambda i: (0, 0))],
        out_specs=[pl.BlockSpec((tm, d // 2), lambda i: (i, 0)),
                   pl.BlockSpec((tm, LANES), lambda i: (i, 0)),
                   pl.BlockSpec((1, LANES), lambda i: (0, 0))],
        out_shape=[jax.ShapeDtypeStruct((t, d // 2), U32),
                   jax.ShapeDtypeStruct((t, LANES), F32),
                   jax.ShapeDtypeStruct((1, LANES), F32)],
        scratch_shapes=[pltpu.VMEM((1, LANES), F32)],
        compiler_params=_cparams("arbitrary"),
    )(x, g.reshape(1, d), w_router, b_router)


def _plan_kernel(cnt_ref, info_ref, dest_ref, meta_ref):
    lane = lax.broadcasted_iota(I32, (8, LANES), 1)
    cnt = jnp.broadcast_to(cnt_ref[...], (8, LANES))
    nblk = jnp.floor((cnt + (MOE_BLOCK - 1)) * (1.0 / MOE_BLOCK))
    r = lax.broadcasted_iota(I32, (LANES, LANES), 0)
    c = lax.broadcasted_iota(I32, (LANES, LANES), 1)
    upto = jnp.where(r <= c, 1.0, 0.0).astype(BF16)
    end_blk = jnp.dot(nblk.astype(BF16), upto, preferred_element_type=F32)
    start_row = (end_blk - nblk) * MOE_BLOCK

    info = info_ref[...]
    tl = lax.broadcasted_iota(I32, info.shape, 1)
    e0, e1, r0, r1 = (info[:, k:k + 1] for k in range(4))
    start = start_row[:1, :]
    d0 = jnp.sum(jnp.where(tl == e0.astype(I32), start, 0.0), axis=1, keepdims=True) + r0
    d1 = jnp.sum(jnp.where(tl == e1.astype(I32), start, 0.0), axis=1, keepdims=True) + r1
    both = jnp.where(tl == 0, d0, jnp.where(tl == 1, d1, 0.0))
    for g in range(both.shape[0] // LANES):
        cols = pl.ds(g * LANES, LANES)
        dest_ref[:, cols] = both[g * LANES:(g + 1) * LANES, :].T[:SUBLANES, :].astype(I32)

    @pl.when(pl.program_id(0) == 0)
    def _():
        nb = meta_ref.shape[0]
        blk = lax.broadcasted_iota(I32, (nb, LANES), 0).astype(F32)
        ml = lax.broadcasted_iota(I32, (nb, LANES), 1)
        done = jnp.where((ml < N_EXPERTS) & (end_blk[:1, :] <= blk), 1.0, 0.0)
        block_e = jnp.minimum(jnp.sum(done, axis=1, keepdims=True), N_EXPERTS - 1.0)
        n_used = jnp.sum(jnp.where(lane[:1, :] == N_EXPERTS - 1, end_blk[:1, :], 0.0), axis=1, keepdims=True)
        meta_ref[...] = jnp.where(ml == 0, block_e, jnp.where(ml == 1, n_used, 0.0)).astype(I32)


def _plan(counts, info, n_blocks):
    t = info.shape[0]
    tm = min(t, 1024)
    nb = -(-n_blocks // 8) * 8
    return pl.pallas_call(
        _plan_kernel,
        grid=(t // tm,),
        in_specs=[pl.BlockSpec((1, LANES), lambda i: (0, 0)), pl.BlockSpec((tm, LANES), lambda i: (i, 0))],
        out_specs=[pl.BlockSpec((SUBLANES, tm), lambda i: (0, i)), pl.BlockSpec((nb, LANES), lambda i: (0, 0))],
        out_shape=[jax.ShapeDtypeStruct((SUBLANES, t), I32), jax.ShapeDtypeStruct((nb, LANES), I32)],
        compiler_params=_cparams("arbitrary"),
    )(counts, info)


def _row_copy(src, src_row, dst, dst_row, n, sem):
    return pltpu.make_async_copy(src.at[pl.ds(src_row, n)], dst.at[pl.ds(dst_row, n)], sem)


def _dispatch_kernel(dest_ref, hp_ref, xs_in_ref, xs_ref, sem, *, tokens_per_step, n_tokens):
    del xs_in_ref
    base = pl.program_id(0) * tokens_per_step

    def issue(j, carry):
        tok = base + j
        _row_copy(hp_ref, j, xs_ref, dest_ref[tok], 1, sem).start()
        _row_copy(hp_ref, j, xs_ref, dest_ref[n_tokens + tok], 1, sem).start()
        return carry

    lax.fori_loop(0, tokens_per_step, issue, 0)
    _row_copy(hp_ref, 0, xs_ref, 0, tokens_per_step, sem).wait()
    _row_copy(hp_ref, 0, xs_ref, 0, tokens_per_step, sem).wait()


def _dispatch(dest, hp, n_rows):
    t, wd = hp.shape
    tc = min(t, 512)
    xs0 = jnp.zeros((n_rows, wd), U32)
    return pl.pallas_call(
        functools.partial(_dispatch_kernel, tokens_per_step=tc, n_tokens=t),
        grid_spec=pltpu.PrefetchScalarGridSpec(
            num_scalar_prefetch=1,
            grid=(t // tc,),
            in_specs=[pl.BlockSpec((tc, wd), lambda i, dr: (i, 0)), pl.BlockSpec(memory_space=pl.ANY)],
            out_specs=pl.BlockSpec(memory_space=pl.ANY),
            scratch_shapes=[pltpu.SemaphoreType.DMA(())],
        ),
        out_shape=jax.ShapeDtypeStruct((n_rows, wd), U32),
        input_output_aliases={2: 0},
        compiler_params=_cparams("arbitrary"),
    )(dest, hp, xs0)


def _expert_changed(be_ref):
    i = pl.program_id(0)
    return (i == 0) | (be_ref[i] != be_ref[jnp.maximum(i - 1, 0)])


def _expert_hidden_kernel(be_ref, nu_ref, xs_ref, wg_ref, wu_ref, o_ref, wg_bf, wu_bf):
    @pl.when(pl.program_id(0) < nu_ref[0])
    def _():
        @pl.when(_expert_changed(be_ref))
        def _():
            wg_bf[...] = wg_ref[0].astype(BF16)
            wu_bf[...] = wu_ref[0].astype(BF16)

        u = xs_ref[...]
        half = u.shape[1]
        lo = pltpu.bitcast(u << 16, F32).astype(BF16)
        hi = pltpu.bitcast(u & jnp.uint32(0xFFFF0000), F32).astype(BF16)

        def proj(w_bf):
            return (jnp.dot(lo, w_bf[:half, :], preferred_element_type=F32)
                    + jnp.dot(hi, w_bf[half:, :], preferred_element_type=F32))

        gate = proj(wg_bf)
        o_ref[...] = (gate * _sigmoid(gate) * proj(wu_bf)).astype(o_ref.dtype)

    @pl.when(pl.program_id(0) >= nu_ref[0])
    def _():
        o_ref[...] = jnp.zeros_like(o_ref)


def _expert_down_kernel(be_ref, nu_ref, h_ref, wd_ref, o_ref, wd_bf):
    @pl.when(pl.program_id(0) < nu_ref[0])
    def _():
        @pl.when(_expert_changed(be_ref))
        def _():
            wd_bf[...] = wd_ref[0].astype(BF16)

        o_ref[...] = jnp.dot(h_ref[...], wd_bf[...], preferred_element_type=F32)

    @pl.when(pl.program_id(0) >= nu_ref[0])
    def _():
        o_ref[...] = jnp.zeros_like(o_ref)


def _experts(block_e, n_used, xs, w_e_gate, w_e_up, w_e_down, layer):
    n_rows, half = xs.shape
    d = 2 * half
    hid = w_e_gate.shape[2]
    n_blocks = n_rows // MOE_BLOCK
    blk = lambda i, be, nu: jnp.minimum(i, nu[0] - 1)
    expert = lambda i, be, nu: (layer * N_EXPERTS + be[blk(i, be, nu)], 0, 0)
    hidden = pl.pallas_call(
        _expert_hidden_kernel,
        grid_spec=pltpu.PrefetchScalarGridSpec(
            num_scalar_prefetch=2,
            grid=(n_blocks,),
            in_specs=[pl.BlockSpec((MOE_BLOCK, half), lambda i, be, nu: (blk(i, be, nu), 0)),
                      pl.BlockSpec((1, d, hid), expert),
                      pl.BlockSpec((1, d, hid), expert)],
            out_specs=pl.BlockSpec((MOE_BLOCK, hid), lambda i, be, nu: (i, 0)),
            scratch_shapes=[pltpu.VMEM((d, hid), BF16), pltpu.VMEM((d, hid), BF16)],
        ),
        out_shape=jax.ShapeDtypeStruct((n_rows, hid), BF16),
        compiler_params=_cparams("arbitrary"),
    )(block_e, n_used, xs, w_e_gate, w_e_up)
    return pl.pallas_call(
        _expert_down_kernel,
        grid_spec=pltpu.PrefetchScalarGridSpec(
            num_scalar_prefetch=2,
            grid=(n_blocks,),
            in_specs=[pl.BlockSpec((MOE_BLOCK, hid), lambda i, be, nu: (blk(i, be, nu), 0)),
                      pl.BlockSpec((1, hid, d), expert)],
            out_specs=pl.BlockSpec((MOE_BLOCK, d), lambda i, be, nu: (i, 0)),
            scratch_shapes=[pltpu.VMEM((hid, d), BF16)],
        ),
        out_shape=jax.ShapeDtypeStruct((n_rows, d), F32),
        compiler_params=_cparams("arbitrary"),
    )(block_e, n_used, hidden, w_e_down)


def _combine_kernel(dest_ref, x_ref, info_ref, g_ref, y_ref, *rest, tokens_per_step, n_tokens, final):
    if final:
        o_ref, buf, sem = rest
    else:
        x2_ref, o_ref, buf, sem = rest
    step = pl.program_id(0)

    def gather(tile, slot):
        def issue(j, carry):
            tok = tile * tokens_per_step + j
            _row_copy(y_ref, dest_ref[tok], buf.at[slot, 0], j, 1, sem.at[slot]).start()
            _row_copy(y_ref, dest_ref[n_tokens + tok], buf.at[slot, 1], j, 1, sem.at[slot]).start()
            return carry

        lax.fori_loop(0, tokens_per_step, issue, 0)

    @pl.when(step == 0)
    def _():
        gather(0, 0)

    @pl.when(step + 1 < pl.num_programs(0))
    def _():
        gather(step + 1, (step + 1) % 2)

    slot = step % 2
    _row_copy(y_ref, 0, buf.at[slot, 0], 0, tokens_per_step, sem.at[slot]).wait()
    _row_copy(y_ref, 0, buf.at[slot, 1], 0, tokens_per_step, sem.at[slot]).wait()
    info = info_ref[...]
    x2 = x_ref[...] + (buf[slot, 0] * info[:, 4:5] + buf[slot, 1] * info[:, 5:6])
    if not final:
        x2_ref[...] = x2
    o_ref[...] = _rms(x2, g_ref[...]).astype(o_ref.dtype)


def _combine(dest, x, info, g_next, y_rows, final):
    t, d = x.shape
    tc = min(t, 256)
    tile = pl.BlockSpec((tc, d), lambda i, dr: (i, 0))
    out_specs = [tile] if final else [tile, tile]
    out_shape = ([jax.ShapeDtypeStruct((t, d), F32)] if final
                 else [jax.ShapeDtypeStruct((t, d), F32), jax.ShapeDtypeStruct((t, d), BF16)])
    return pl.pallas_call(
        functools.partial(_combine_kernel, tokens_per_step=tc, n_tokens=t, final=final),
        grid_spec=pltpu.PrefetchScalarGridSpec(
            num_scalar_prefetch=1,
            grid=(t // tc,),
            in_specs=[tile,
                      pl.BlockSpec((tc, LANES), lambda i, dr: (i, 0)),
                      pl.BlockSpec((1, d), lambda i, dr: (0, 0)),
                      pl.BlockSpec(memory_space=pl.ANY)],
            out_specs=out_specs,
            scratch_shapes=[pltpu.VMEM((2, 2, tc, d), F32), pltpu.SemaphoreType.DMA((2,))],
        ),
        out_shape=out_shape,
        compiler_params=_cparams("arbitrary"),
    )(dest, x, info, g_next.reshape(1, d), y_rows)


def _pad_cols(w, width):
    return jnp.pad(w, ((0, 0), (0, width - w.shape[1])))


def _stage_w_in(w_in, d):
    dsa_heads = d // (4 * HEAD_DIM)
    hpg = d // 1024
    dil_heads = len(DIL_PATTERNS) * hpg
    a_w, b_w = dsa_heads * HEAD_DIM, dil_heads * HEAD_DIM
    pool_w = d - a_w - b_w
    sizes = (a_w, HEAD_DIM, HEAD_DIM, IDX_HEADS * IDX_DIM, IDX_DIM, IDX_HEADS, b_w, b_w, b_w, pool_w)
    offs = [0]
    for s in sizes:
        offs.append(offs[-1] + s)
    cut = offs[6]
    shift = -cut % MXU_WIDTH
    w_al = jnp.concatenate([w_in[:, :cut].astype(BF16), jnp.zeros((w_in.shape[0], shift), BF16),
                            w_in[:, cut:].astype(BF16)], axis=1)
    start = [o + (shift if n >= 6 else 0) for n, o in enumerate(offs)]
    qa, ka, va, qi, ki, wi, qb, kb, vb, c, end = start
    assert all(o % MXU_WIDTH == 0 for o in (qa, ka, qi, ki, qb, kb, vb, c, end))
    assert va == ka + HEAD_DIM and qi == ka + MXU_WIDTH and qb == ki + MXU_WIDTH and wi == ki + IDX_DIM
    tiles = lambda lo, hi: list(range(lo // MXU_WIDTH, hi // MXU_WIDTH))
    t_kav, t_kiw = tiles(ka, qi), tiles(ki, qb)
    calls = dict(rope=tiles(qa, ka) + t_kav + tiles(qb, vb), idx=tiles(qi, ki) + t_kiw,
                 val=t_kav + tiles(vb, c), pool=tiles(c, end) + t_kiw)
    lay = dict(dsa_heads=dsa_heads, dil_heads_per_group=hpg,
               ka=a_w, qb=a_w + MXU_WIDTH, kb=a_w + MXU_WIDTH + b_w,
               ki=IDX_HEADS * IDX_DIM,
               va=HEAD_DIM, vb=MXU_WIDTH,
               wi=pool_w, wi_lane=IDX_DIM)
    return w_al, calls, lay


def kernel(x, positions, norm1_g, w_in, w_gate, b_gate, w_br_a, w_br_b, w_br_c, w_pool, pool_scale,
           w_out, norm2_g, w_rg, b_rg, w_re, b_re, w_e_gate, w_e_up, w_e_down, final_g):
    bsz, seq, d = x.shape
    depth = w_in.shape[0]
    t = bsz * seq
    n_blocks = -(-(2 * t) // MOE_BLOCK) + N_EXPERTS
    n_rows = n_blocks * MOE_BLOCK

    c128, s128, c64, s64 = _rope_tables(positions)
    xt = x.reshape(t, d)
    h = _rms_norm(xt, norm1_g[0], BF16)
    out = None
    for l in range(depth):
        w_al, tiles, lay = _stage_w_in(w_in[l], d)
        zr = _project(h, w_al, tiles["rope"], BF16, rope=(HEAD_DIM, c128, s128))
        zi = _project(h, w_al, tiles["idx"], BF16, rope=(IDX_DIM, c64, s64))
        zv = _project(h, w_al, tiles["val"], BF16)
        zc = _project(h, w_al, tiles["pool"], F32)

        a_out = _dsa(zr, zi, zv, zc, bsz, seq, lay)
        b_out = _dilated(zr, zv, bsz, seq, lay)
        c_out = _pool(zc, w_pool[l].astype(BF16), pool_scale[l], bsz, seq)

        merged = _merge(h, w_gate[l].astype(BF16), b_gate[l], a_out, b_out, c_out,
                        w_br_a[l].astype(BF16), w_br_b[l].astype(BF16), w_br_c[l].astype(BF16))
        x1 = _outproj(merged, w_out[l].astype(BF16), xt)

        w_router = _pad_cols(jnp.concatenate([w_re[l], w_rg[l]], axis=1), LANES).astype(BF16)
        b_router = _pad_cols(jnp.concatenate([b_re[l], b_rg[l]])[None, :], LANES)
        hp, info, counts = _router(x1, norm2_g[l], w_router, b_router)
        dest2, meta = _plan(counts, info, n_blocks)
        dest = dest2[:2].reshape(2 * t)
        block_e = meta[:n_blocks, 0]
        n_used = meta[:1, 1]
        xs = _dispatch(dest, hp, n_rows)
        stack = lambda w: w.reshape((depth * N_EXPERTS,) + w.shape[2:])
        y_rows = _experts(block_e, n_used, xs, stack(w_e_gate), stack(w_e_up), stack(w_e_down), l)
        final = l == depth - 1
        g_next = final_g if final else norm1_g[l + 1]
        res = _combine(dest, x1, info, g_next, y_rows, final)
        if final:
            out = res[0]
        else:
            xt, h = res
    return out.reshape(bsz, seq, d)
```

```python
import functools

import jax
import jax.numpy as jnp
from jax import lax
from jax.experimental import pallas as pl
from jax.experimental.pallas import tpu as pltpu

F32 = jnp.float32
BF16 = jnp.bfloat16
I32 = jnp.int32
U32 = jnp.uint32

LANES = 128
MXU_WIDTH = 256
VMEM_LIMIT = 56 * 1024 * 1024

HEAD_DIM = 128
ROPE_THETA = 500000.0
ROPE_FRACTION = 4
NORM_EPS = 1e-6
IDX_HEADS = 16
IDX_DIM = 64
DSA_TOPK = 256
Q_BLOCK = 128
DIL_PATTERNS = ((128, 1), (512, 4), (2048, 16))
POOL_WINDOWS = (2, 4, 8, 16)
N_EXPERT_GROUPS = 4
EXPERTS_PER_GROUP = 8
N_EXPERTS = N_EXPERT_GROUPS * EXPERTS_PER_GROUP
MOE_BLOCK = 512
NEG_INF = float("-inf")


def _cparams(*sem):
    return pltpu.CompilerParams(dimension_semantics=sem, vmem_limit_bytes=VMEM_LIMIT)


def _rope_table_kernel(pos_ref, f128_ref, s128_ref, f64_ref, s64_ref, c128, sn128, c64, sn64):
    p = pos_ref[...].astype(F32)
    a = p * f128_ref[...]
    c128[...] = jnp.cos(a)
    sn128[...] = jnp.sin(a) * s128_ref[...]
    a = p * f64_ref[...]
    c64[...] = jnp.cos(a)
    sn64[...] = jnp.sin(a) * s64_ref[...]


def _rope_lane_patterns():
    lane = jnp.arange(LANES)

    def pattern(width):
        rd = width // ROPE_FRACTION
        half = rd // 2
        inv = ROPE_THETA ** (-jnp.arange(half, dtype=F32) * (2.0 / rd))
        l = lane % width
        freq = jnp.where(l < rd, inv[l % half], 0.0).astype(F32)
        sign = jnp.where(l < half, -1.0, jnp.where(l < rd, 1.0, 0.0)).astype(F32)
        return freq[None, :], sign[None, :]

    return pattern(HEAD_DIM) + pattern(IDX_DIM)


def _rope_tables(positions):
    t = positions.size
    tm = min(t, 1024)
    f128, s128, f64, s64 = _rope_lane_patterns()
    vec = pl.BlockSpec((1, LANES), lambda i: (0, 0))
    tab = pl.BlockSpec((tm, LANES), lambda i: (i, 0))
    return pl.pallas_call(
        _rope_table_kernel,
        grid=(t // tm,),
        in_specs=[pl.BlockSpec((tm, 1), lambda i: (i, 0)), vec, vec, vec, vec],
        out_specs=[tab] * 4,
        out_shape=[jax.ShapeDtypeStruct((t, LANES), F32)] * 4,
        compiler_params=_cparams("arbitrary"),
    )(positions.reshape(t, 1), f128, s128, f64, s64)


def _rms(x, g):
    ms = jnp.mean(x * x, axis=-1, keepdims=True)
    return x * lax.rsqrt(ms + NORM_EPS) * g


def _norm_kernel(x_ref, g_ref, o_ref):
    o_ref[...] = _rms(x_ref[...], g_ref[...]).astype(o_ref.dtype)


def _rms_norm(x, g, out_dtype):
    t, d = x.shape
    tm = min(t, 512)
    return pl.pallas_call(
        _norm_kernel,
        grid=(t // tm,),
        in_specs=[pl.BlockSpec((tm, d), lambda i: (i, 0)), pl.BlockSpec((1, d), lambda i: (0, 0))],
        out_specs=pl.BlockSpec((tm, d), lambda i: (i, 0)),
        out_shape=jax.ShapeDtypeStruct((t, d), out_dtype),
        compiler_params=_cparams("arbitrary"),
    )(x, g.reshape(1, d))


PROJ_SUB_ROWS = 512


def _proj_kernel(tiles_ref, x_ref, w_ref, *rest, rope_width):
    del tiles_ref
    o_ref = rest[-1]
    sub = min(PROJ_SUB_ROWS, x_ref.shape[0])
    for r in range(x_ref.shape[0] // sub):
        rows = pl.ds(r * sub, sub)
        acc = jnp.dot(x_ref[rows, :], w_ref[...], preferred_element_type=F32)
        if rope_width is None:
            o_ref[rows, :] = acc.astype(o_ref.dtype)
            continue
        c_ref, s_ref = rest[:2]
        half = rope_width // ROPE_FRACTION // 2
        c = c_ref[rows, :]
        s = s_ref[rows, :]
        lane = lax.broadcasted_iota(I32, c.shape, 1)
        first_half = (lane % rope_width) < half
        for g in range(acc.shape[1] // LANES):
            z = acc[:, g * LANES:(g + 1) * LANES]
            partner = jnp.where(first_half, pltpu.roll(z, LANES - half, 1), pltpu.roll(z, half, 1))
            o_ref[rows, g * LANES:(g + 1) * LANES] = (z * c + partner * s).astype(o_ref.dtype)


def _project(h, w, tiles, out_dtype, rope=None):
    t, k = h.shape
    tn = MXU_WIDTH
    tm = min(t, 2048)
    in_specs = [pl.BlockSpec((tm, k), lambda i, j, tl: (i, 0)),
                pl.BlockSpec((k, tn), lambda i, j, tl: (0, tl[j]))]
    args = [h, w]
    if rope is not None:
        tab = pl.BlockSpec((tm, LANES), lambda i, j, tl: (i, 0))
        in_specs += [tab, tab]
        args += [rope[1], rope[2]]
    return pl.pallas_call(
        functools.partial(_proj_kernel, rope_width=None if rope is None else rope[0]),
        grid_spec=pltpu.PrefetchScalarGridSpec(
            num_scalar_prefetch=1,
            grid=(t // tm, len(tiles)),
            in_specs=in_specs,
            out_specs=pl.BlockSpec((tm, tn), lambda i, j, tl: (i, j)),
        ),
        out_shape=jax.ShapeDtypeStruct((t, len(tiles) * tn), out_dtype),
        compiler_params=_cparams("arbitrary", "arbitrary"),
    )(jnp.asarray(tiles, I32), *args)


KEY_CHUNK = 256
SUBLANES = 8


def _sublane_allreduce(x, op):
    for shift in (4, 2, 1):
        x = op(x, pltpu.roll(x, shift, 0))
    return x


def _tree_reduce(x, op):
    while x.shape[0] > 1:
        half = x.shape[0] // 2
        x = op(x[:half], x[half:])
    return x[0]


def _vregs(x):
    return x.reshape(x.shape[0] // SUBLANES, SUBLANES, LANES)


def _dsa_kernel(qa_ref, ka_ref, va_ref, qi_ref, ki_ref, wi_ref, o_ref,
                key_ref, vat_ref, kil_ref, kih_ref, acc_ref, m_ref, l_ref, *, topk, n_heads, wi_lane):
    nq = o_ref.shape[0]
    ck = KEY_CHUNK
    i = pl.program_id(1)
    t0 = i * nq
    n_chunks = (t0 + nq + ck - 1) // ck
    nt = (((1,), (1,)), ((), ()))

    @pl.when(i == 0)
    def _():
        for c in range(vat_ref.shape[0]):
            vat_ref[c] = va_ref[c * ck:(c + 1) * ck, :].astype(F32).T.astype(BF16)
        ki = ki_ref[...].astype(F32)
        ki = jnp.where(lax.broadcasted_iota(I32, ki.shape, 1) < IDX_DIM, ki, 0.0)
        kil_ref[...] = ki.astype(BF16)
        kih_ref[...] = pltpu.roll(ki, IDX_DIM, 1).astype(BF16)

    def causal_mask(c):
        shape = (ck // SUBLANES, SUBLANES, nq)
        spos = (c * ck + lax.broadcasted_iota(I32, shape, 0) * SUBLANES
                + lax.broadcasted_iota(I32, shape, 1))
        tpos = t0 + lax.broadcasted_iota(I32, shape, 2)
        return spos <= tpos

    wt = (wi_ref[...] * ((IDX_DIM ** -0.5) * (IDX_HEADS ** -0.5))).T
    w8 = [jnp.broadcast_to(wt[wi_lane + h:wi_lane + h + 1, :], (SUBLANES, LANES)) for h in range(IDX_HEADS)]
    q_stack = [jnp.concatenate([qi_ref[:, (2 * m) * LANES:(2 * m + 1) * LANES],
                                qi_ref[:, (2 * m + 1) * LANES:(2 * m + 2) * LANES]], axis=0)
               for m in range(IDX_HEADS // 4)]

    def score_chunk(c, carry):
        r0 = pl.multiple_of(c * ck, ck)
        kil = kil_ref[pl.ds(r0, ck), :]
        kih = kih_ref[pl.ds(r0, ck), :]
        acc = jnp.zeros((ck // SUBLANES, SUBLANES, LANES), F32)
        for m in range(IDX_HEADS // 4):
            d_even = lax.dot_general(kil, q_stack[m], nt, preferred_element_type=F32)
            d_odd = lax.dot_general(kih, q_stack[m], nt, preferred_element_type=F32)
            for dots, h in ((d_even[:, :nq], 4 * m), (d_odd[:, :nq], 4 * m + 1),
                            (d_even[:, nq:], 4 * m + 2), (d_odd[:, nq:], 4 * m + 3)):
                acc = acc + jnp.maximum(_vregs(dots), 0.0) * w8[h][None]
        score = jnp.where(causal_mask(c), acc, NEG_INF).reshape(ck, nq)
        bits = pltpu.bitcast(score, I32)
        key_ref[c] = bits ^ ((bits >> 31) & jnp.int32(0x7FFFFFFF))
        return carry

    lax.fori_loop(0, n_chunks, score_chunk, 0)

    def count(compare, cand):
        def body(c, cnt):
            return cnt + _tree_reduce(jnp.where(compare(_vregs(key_ref[c]), cand[None]), 1.0, 0.0), jnp.add)
        cnt = lax.fori_loop(0, n_chunks, body, jnp.zeros((SUBLANES, LANES), F32))
        return _sublane_allreduce(cnt, jnp.add)

    count_ge = functools.partial(count, jnp.greater_equal)
    int_min = jnp.int32(-2 ** 31)
    thr = jnp.where(count_ge(jnp.zeros((SUBLANES, LANES), I32)) >= topk, jnp.int32(0), int_min)

    def refine(it, thr):
        cand = thr | (jnp.int32(1) << (30 - it))
        return jnp.where(count_ge(cand) >= topk, cand, thr)

    thr = lax.fori_loop(0, 31, refine, thr)
    tie_slots = topk - count(jnp.greater, thr)
    r = lax.broadcasted_iota(I32, (ck, ck), 0)
    cidx = lax.broadcasted_iota(I32, (ck, ck), 1)
    earlier = jnp.where(cidx < r, 1.0, 0.0).astype(BF16)

    qt = jnp.concatenate([qa_ref[:, h * HEAD_DIM:(h + 1) * HEAD_DIM].astype(F32).T for h in range(n_heads)],
                         axis=1).astype(BF16)
    m_ref[...] = jnp.full(m_ref.shape, NEG_INF, F32)
    l_ref[...] = jnp.zeros(l_ref.shape, F32)
    acc_ref[...] = jnp.zeros(acc_ref.shape, F32)

    def attend(c, ties_seen):
        r0 = pl.multiple_of(c * ck, ck)
        key = _vregs(key_ref[c])
        tie = jnp.where(key == thr[None], 1.0, 0.0)
        ties_before = _vregs(jnp.dot(earlier, tie.reshape(ck, nq).astype(BF16),
                                     preferred_element_type=F32)) + ties_seen[None]
        tie_bias = jnp.where(ties_before < tie_slots[None], tie - 1.0, NEG_INF)
        bias = jnp.where(key > thr[None], 0.0, jnp.where(tie > 0.0, tie_bias, NEG_INF))
        bias = jnp.where(causal_mask(c), bias, NEG_INF)
        ties_seen = ties_seen + _sublane_allreduce(_tree_reduce(tie, jnp.add), jnp.add)
        s_all = jnp.dot(ka_ref[pl.ds(r0, ck), :], qt, preferred_element_type=F32) * (HEAD_DIM ** -0.5)
        ps, alphas = [], []
        for h in range(n_heads):
            s = _vregs(s_all[:, h * nq:(h + 1) * nq]) + bias
            m_old = m_ref[h]
            m_new = jnp.maximum(m_old, _sublane_allreduce(_tree_reduce(s, jnp.maximum), jnp.maximum))
            m_safe = jnp.where(m_new == NEG_INF, 0.0, m_new)
            alphas.append(jnp.exp(m_old - m_safe))
            p = jnp.exp(s - m_safe[None])
            l_ref[h] = alphas[h] * l_ref[h] + _tree_reduce(p, jnp.add)
            ps.append(p.reshape(ck, nq).astype(BF16))
            m_ref[h] = m_new
        pv = jnp.dot(vat_ref[c], jnp.concatenate(ps, axis=1), preferred_element_type=F32)
        for h in range(n_heads):
            acc_ref[h] = (_vregs(acc_ref[h]) * alphas[h][None]).reshape(HEAD_DIM, nq) + pv[:, h * nq:(h + 1) * nq]
        return ties_seen

    lax.fori_loop(0, n_chunks, attend, jnp.zeros((SUBLANES, LANES), F32))
    for h in range(n_heads):
        l = _sublane_allreduce(l_ref[h], jnp.add)
        o = (_vregs(acc_ref[h]) / l[None]).reshape(HEAD_DIM, nq)
        o_ref[:, h * HEAD_DIM:(h + 1) * HEAD_DIM] = o.T.astype(o_ref.dtype)


def _dsa(zr, zi, zv, zc, bsz, seq, lay):
    t = bsz * seq
    nqb = seq // Q_BLOCK
    n_heads = lay["dsa_heads"]
    a_w = n_heads * HEAD_DIM
    qw = IDX_HEADS * IDX_DIM
    nck = seq // KEY_CHUNK
    row = lambda b, i: b * nqb + i
    in_specs = [
        pl.BlockSpec((Q_BLOCK, a_w), lambda b, i: (row(b, i), 0)),
        pl.BlockSpec((seq, HEAD_DIM), lambda b, i: (b, lay["ka"] // HEAD_DIM)),
        pl.BlockSpec((seq, HEAD_DIM), lambda b, i: (b, lay["va"] // HEAD_DIM)),
        pl.BlockSpec((Q_BLOCK, qw), lambda b, i: (row(b, i), 0)),
        pl.BlockSpec((seq, LANES), lambda b, i: (b, lay["ki"] // LANES)),
        pl.BlockSpec((Q_BLOCK, LANES), lambda b, i: (row(b, i), lay["wi"] // LANES)),
    ]
    return pl.pallas_call(
        functools.partial(_dsa_kernel, topk=min(DSA_TOPK, seq // 4), n_heads=n_heads, wi_lane=lay["wi_lane"]),
        grid=(bsz, nqb),
        in_specs=in_specs,
        out_specs=pl.BlockSpec((Q_BLOCK, a_w), lambda b, i: (row(b, i), 0)),
        out_shape=jax.ShapeDtypeStruct((t, a_w), BF16),
        scratch_shapes=[pltpu.VMEM((nck, KEY_CHUNK, Q_BLOCK), I32),
                        pltpu.VMEM((nck, HEAD_DIM, KEY_CHUNK), BF16),
                        pltpu.VMEM((seq, LANES), BF16),
                        pltpu.VMEM((seq, LANES), BF16),
                        pltpu.VMEM((n_heads, HEAD_DIM, Q_BLOCK), F32),
                        pltpu.VMEM((n_heads, SUBLANES, Q_BLOCK), F32),
                        pltpu.VMEM((n_heads, SUBLANES, Q_BLOCK), F32)],
        compiler_params=_cparams("arbitrary", "arbitrary"),
    )(zr, zr, zv, zi, zi, zc)


MIX_ROWS = 256


def _band_attention(q, k_blocks, v_blocks):
    bs = q.shape[0]
    nt = (((1,), (1,)), ((), ()))
    k = k_blocks[0] if len(k_blocks) == 1 else jnp.concatenate(k_blocks, axis=0)
    v = v_blocks[0] if len(v_blocks) == 1 else jnp.concatenate(v_blocks, axis=0)
    s = lax.dot_general(q, k, nt, preferred_element_type=F32) * (HEAD_DIM ** -0.5)
    a = lax.broadcasted_iota(I32, s.shape, 0)
    c = lax.broadcasted_iota(I32, s.shape, 1)
    if len(k_blocks) == 1:
        valid = c <= a
    else:
        valid = (a + bs - c).astype(U32) <= jnp.uint32(bs)
    s = jnp.where(valid, s, NEG_INF)
    m = jnp.max(s, axis=1, keepdims=True)
    p = jnp.exp(s - m)
    l = jnp.sum(p, axis=1, keepdims=True)
    o = jnp.dot(p.astype(BF16), v, preferred_element_type=F32) / l
    return o, m + jnp.log(l)


def _dil_kernel(*refs, seq):
    ng = len(DIL_PATTERNS)
    q_refs, k_refs, v_refs = refs[:ng], refs[ng:2 * ng], refs[2 * ng:3 * ng]
    o_ref, qf, kf, vf, of, lf = refs[3 * ng:]
    for g, (window, dil) in enumerate(DIL_PATTERNS):
        ls = seq // dil
        bs = min(ls, Q_BLOCK)
        assert ls % bs == 0 and window // dil == Q_BLOCK
        if dil > 1:
            qf[...] = q_refs[g][...].astype(F32)
            kf[...] = k_refs[g][...].astype(F32)
            vf[...] = v_refs[g][...].astype(F32)
        for x in range(dil):
            def rows(n):
                return pl.ds(x + dil * bs * n, bs, stride=dil) if dil > 1 else pl.ds(bs * n, bs)

            def load(src_bf16, src_f32, n):
                return src_bf16[rows(n), :] if dil == 1 else src_f32[rows(n), :].astype(BF16)

            k_prev = v_prev = None
            for n in range(ls // bs):
                q = load(q_refs[g], qf, n)
                k_cur = load(k_refs[g], kf, n)
                v_cur = load(v_refs[g], vf, n)
                if n == 0:
                    o, lse = _band_attention(q, [k_cur], [v_cur])
                else:
                    o, lse = _band_attention(q, [k_prev, k_cur], [v_prev, v_cur])
                of[g, rows(n), :] = o
                lf[g, rows(n), :] = jnp.broadcast_to(lse, o.shape)
                k_prev, v_prev = k_cur, v_cur
    mr = min(MIX_ROWS, seq)
    for r in range(seq // mr):
        sl = pl.ds(r * mr, mr)
        lses = [lf[g, sl, :] for g in range(ng)]
        mx = functools.reduce(jnp.maximum, lses)
        es = [jnp.exp(x - mx) for x in lses]
        tot = functools.reduce(lambda u, w: u + w, es)
        acc = functools.reduce(lambda u, w: u + w, [(e / tot) * of[g, sl, :] for g, e in enumerate(es)])
        o_ref[sl, :] = acc.astype(o_ref.dtype)


def _dilated(zr, zv, bsz, seq, lay):
    t = bsz * seq
    hpg = lay["dil_heads_per_group"]
    ng = len(DIL_PATTERNS)

    def head_spec(base, g):
        return pl.BlockSpec((seq, HEAD_DIM), lambda b, s: (b, base // HEAD_DIM + g * hpg + s))

    in_specs = ([head_spec(lay["qb"], g) for g in range(ng)]
                + [head_spec(lay["kb"], g) for g in range(ng)]
                + [head_spec(lay["vb"], g) for g in range(ng)])
    return pl.pallas_call(
        functools.partial(_dil_kernel, seq=seq),
        grid=(bsz, hpg),
        in_specs=in_specs,
        out_specs=pl.BlockSpec((seq, HEAD_DIM), lambda b, s: (b, s)),
        out_shape=jax.ShapeDtypeStruct((t, hpg * HEAD_DIM), BF16),
        scratch_shapes=[pltpu.VMEM((seq, HEAD_DIM), F32)] * 3
                       + [pltpu.VMEM((ng, seq, HEAD_DIM), F32)] * 2,
        compiler_params=_cparams("arbitrary", "arbitrary"),
    )(*([zr] * (2 * ng) + [zv] * ng))


def _pool_kernel(c_ref, wp_ref, ps_ref, o_ref):
    x = c_ref[...]
    row = lax.broadcasted_iota(I32, x.shape, 0)

    def run(window):
        acc = x
        k = 1
        while k < window:
            acc = acc + jnp.where(row >= k, pltpu.roll(acc, k, 0), 0.0)
            k *= 2
        denom = jnp.minimum(row[:, :1] + 1, window).astype(F32)
        y = (acc / denom - x).astype(BF16)
        o = jnp.dot(y, wp_ref[0], preferred_element_type=F32) * ps_ref[...]
        o_ref[...] = o.astype(o_ref.dtype)

    for gi, window in enumerate(POOL_WINDOWS):
        pl.when(pl.program_id(1) == gi)(functools.partial(run, window))


def _pool(zc, w_pool, pool_scale, bsz, seq):
    t = bsz * seq
    ng, gw = w_pool.shape[0], w_pool.shape[1]
    return pl.pallas_call(
        _pool_kernel,
        grid=(bsz, ng),
        in_specs=[pl.BlockSpec((seq, gw), lambda b, g: (b, g)),
                  pl.BlockSpec((1, gw, gw), lambda b, g: (g, 0, 0)),
                  pl.BlockSpec((1, gw), lambda b, g: (0, g))],
        out_specs=pl.BlockSpec((seq, gw), lambda b, g: (b, g)),
        out_shape=jax.ShapeDtypeStruct((t, ng * gw), BF16),
        compiler_params=_cparams("arbitrary", "arbitrary"),
    )(zc, w_pool, pool_scale.reshape(1, ng * gw))


def _sigmoid(x):
    return 1.0 / (1.0 + jnp.exp(-x))


def _merge_kernel(h_ref, wg0, wg1, wg2, bg0, bg1, bg2, a_ref, b_ref, c_ref, wa, wb, wc, o_ref):
    h = h_ref[...]
    acc = None
    for wg, bg, br, wbr in ((wg0, bg0, a_ref, wa), (wg1, bg1, b_ref, wb), (wg2, bg2, c_ref, wc)):
        gate = _sigmoid(jnp.dot(h, wg[...], preferred_element_type=F32) + bg[...])
        term = gate * jnp.dot(br[...], wbr[...], preferred_element_type=F32)
        acc = term if acc is None else acc + term
    o_ref[...] = acc.astype(o_ref.dtype)


def _merge(h, w_gate, b_gate, a_out, b_out, c_out, w_a, w_b, w_c):
    t, d = h.shape
    tm = min(t, 1024)
    tn = 256
    nd = d // tn
    rows = lambda width: pl.BlockSpec((tm, width), lambda i, j: (i, 0))
    cols = lambda k, off: pl.BlockSpec((k, tn), lambda i, j: (0, off * nd + j))
    in_specs = ([rows(d)] + [cols(d, n) for n in range(3)] + [cols(1, n) for n in range(3)]
                + [rows(a_out.shape[1]), rows(b_out.shape[1]), rows(c_out.shape[1])]
                + [cols(a_out.shape[1], 0), cols(b_out.shape[1], 0), cols(c_out.shape[1], 0)])
    bg = b_gate.reshape(1, 3 * d)
    return pl.pallas_call(
        _merge_kernel,
        grid=(t // tm, nd),
        in_specs=in_specs,
        out_specs=pl.BlockSpec((tm, tn), lambda i, j: (i, j)),
        out_shape=jax.ShapeDtypeStruct((t, d), BF16),
        compiler_params=_cparams("arbitrary", "arbitrary"),
    )(h, w_gate, w_gate, w_gate, bg, bg, bg, a_out, b_out, c_out, w_a, w_b, w_c)


def _outproj_kernel(m_ref, w_ref, x_ref, o_ref):
    o_ref[...] = x_ref[...] + jnp.dot(m_ref[...], w_ref[...], preferred_element_type=F32)


def _outproj(merged, w_out, x):
    t, d = x.shape
    tm = min(t, 1024)
    tn = 512
    return pl.pallas_call(
        _outproj_kernel,
        grid=(t // tm, d // tn),
        in_specs=[pl.BlockSpec((tm, d), lambda i, j: (i, 0)),
                  pl.BlockSpec((d, tn), lambda i, j: (0, j)),
                  pl.BlockSpec((tm, tn), lambda i, j: (i, j))],
        out_specs=pl.BlockSpec((tm, tn), lambda i, j: (i, j)),
        out_shape=jax.ShapeDtypeStruct((t, d), F32),
        compiler_params=_cparams("arbitrary", "arbitrary"),
    )(merged, w_out, x)


GROUP_LANE0 = N_EXPERTS


def _lane_min_index(mask, lane):
    return jnp.min(jnp.where(mask, lane, LANES), axis=1, keepdims=True)


def _router_kernel(x_ref, g_ref, wr_ref, br_ref, hp_ref, info_ref, cnt_ref, carry_ref):
    @pl.when(pl.program_id(0) == 0)
    def _():
        carry_ref[...] = jnp.zeros_like(carry_ref)

    hb = _rms(x_ref[...], g_ref[...]).astype(BF16)
    tm, d = hb.shape
    lo = pltpu.bitcast(hb[:, :d // 2].astype(F32), U32)
    hi = pltpu.bitcast(hb[:, d // 2:].astype(F32), U32)
    hp_ref[...] = hi | (lo >> 16)

    logits = jnp.dot(hb, wr_ref[...], preferred_element_type=F32) + br_ref[...]
    lane = lax.broadcasted_iota(I32, logits.shape, 1)
    is_group = (lane >= GROUP_LANE0) & (lane < GROUP_LANE0 + N_EXPERT_GROUPS)
    lg = jnp.where(is_group, logits, NEG_INF)
    gmax = jnp.max(lg, axis=1, keepdims=True)
    gsel = _lane_min_index(lg == gmax, lane) - GROUP_LANE0
    pg_sel = 1.0 / jnp.sum(jnp.where(is_group, jnp.exp(lg - gmax), 0.0), axis=1, keepdims=True)
    in_group = (lane < N_EXPERTS) & ((lane // EXPERTS_PER_GROUP) == gsel)
    le = jnp.where(in_group, logits, NEG_INF)
    ee = jnp.where(in_group, jnp.exp(le - jnp.max(le, axis=1, keepdims=True)), 0.0)
    pe = jnp.where(in_group, ee / jnp.sum(ee, axis=1, keepdims=True), -1.0)
    v0 = jnp.max(pe, axis=1, keepdims=True)
    e0 = _lane_min_index(pe == v0, lane)
    pe1 = jnp.where(lane == e0, -1.0, pe)
    v1 = jnp.max(pe1, axis=1, keepdims=True)
    e1 = _lane_min_index(pe1 == v1, lane)
    w0 = pg_sel * v0 / (v0 + v1)
    w1 = pg_sel * v1 / (v0 + v1)
    onehot = jnp.where((lane == e0) | (lane == e1), 1.0, 0.0)
    r = lax.broadcasted_iota(I32, (tm, tm), 0)
    c = lax.broadcasted_iota(I32, (tm, tm), 1)
    before = jnp.where(c < r, 1.0, 0.0).astype(BF16)
    prior = jnp.dot(before, onehot.astype(BF16), preferred_element_type=F32) + carry_ref[...]
    r0 = jnp.sum(jnp.where(lane == e0, prior, 0.0), axis=1, keepdims=True)
    r1 = jnp.sum(jnp.where(lane == e1, prior, 0.0), axis=1, keepdims=True)
    carry_ref[...] = carry_ref[...] + jnp.sum(onehot, axis=0, keepdims=True)
    cnt_ref[...] = carry_ref[...]
    info = jnp.zeros_like(logits)
    for k, val in enumerate((e0.astype(F32), e1.astype(F32), r0, r1, w0, w1)):
        info = jnp.where(lane == k, val, info)
    info_ref[...] = info


def _router(x, g, w_router, b_router):
    t, d = x.shape
    tm = min(t, 256)
    return pl.pallas_call(
        _router_kernel,
        grid=(t // tm,),
        in_specs=[pl.BlockSpec((tm, d), lambda i: (i, 0)),
                  pl.BlockSpec((1, d), lambda i: (0, 0)),
                  pl.BlockSpec((d, LANES), lambda i: (0, 0)),
                  pl.BlockSpec((1, LANES), lambda i: (0, 0))],
        out_specs=[pl.BlockSpec((tm, d // 2), lambda i: (i, 0)),
                   pl.BlockSpec((tm, LANES), lambda i: (i, 0)),
                   pl.BlockSpec((1, LANES), lambda i: (0, 0))],
        out_shape=[jax.ShapeDtypeStruct((t, d // 2), U32),
                   jax.ShapeDtypeStruct((t, LANES), F32),
                   jax.ShapeDtypeStruct((1, LANES), F32)],
        scratch_shapes=[pltpu.VMEM((1, LANES), F32)],
        compiler_params=_cparams("arbitrary"),
    )(x, g.reshape(1, d), w_router, b_router)


def _plan_kernel(cnt_ref, info_ref, dest_ref, meta_ref):
    lane = lax.broadcasted_iota(I32, (8, LANES), 1)
    cnt = jnp.broadcast_to(cnt_ref[...], (8, LANES))
    nblk = jnp.floor((cnt + (MOE_BLOCK - 1)) * (1.0 / MOE_BLOCK))
    r = lax.broadcasted_iota(I32, (LANES, LANES), 0)
    c = lax.broadcasted_iota(I32, (LANES, LANES), 1)
    upto = jnp.where(r <= c, 1.0, 0.0).astype(BF16)
    end_blk = jnp.dot(nblk.astype(BF16), upto, preferred_element_type=F32)
    start_row = (end_blk - nblk) * MOE_BLOCK

    info = info_ref[...]
    tl = lax.broadcasted_iota(I32, info.shape, 1)
    e0, e1, r0, r1 = (info[:, k:k + 1] for k in range(4))
    start = start_row[:1, :]
    d0 = jnp.sum(jnp.where(tl == e0.astype(I32), start, 0.0), axis=1, keepdims=True) + r0
    d1 = jnp.sum(jnp.where(tl == e1.astype(I32), start, 0.0), axis=1, keepdims=True) + r1
    both = jnp.where(tl == 0, d0, jnp.where(tl == 1, d1, 0.0))
    for g in range(both.shape[0] // LANES):
        cols = pl.ds(g * LANES, LANES)
        dest_ref[:, cols] = both[g * LANES:(g + 1) * LANES, :].T[:SUBLANES, :].astype(I32)

    @pl.when(pl.program_id(0) == 0)
    def _():
        nb = meta_ref.shape[0]
        blk = lax.broadcasted_iota(I32, (nb, LANES), 0).astype(F32)
        ml = lax.broadcasted_iota(I32, (nb, LANES), 1)
        done = jnp.where((ml < N_EXPERTS) & (end_blk[:1, :] <= blk), 1.0, 0.0)
        block_e = jnp.minimum(jnp.sum(done, axis=1, keepdims=True), N_EXPERTS - 1.0)
        n_used = jnp.sum(jnp.where(lane[:1, :] == N_EXPERTS - 1, end_blk[:1, :], 0.0), axis=1, keepdims=True)
        meta_ref[...] = jnp.where(ml == 0, block_e, jnp.where(ml == 1, n_used, 0.0)).astype(I32)


def _plan(counts, info, n_blocks):
    t = info.shape[0]
    tm = min(t, 1024)
    nb = -(-n_blocks // 8) * 8
    return pl.pallas_call(
        _plan_kernel,
        grid=(t // tm,),
        in_specs=[pl.BlockSpec((1, LANES), lambda i: (0, 0)), pl.BlockSpec((tm, LANES), lambda i: (i, 0))],
        out_specs=[pl.BlockSpec((SUBLANES, tm), lambda i: (0, i)), pl.BlockSpec((nb, LANES), lambda i: (0, 0))],
        out_shape=[jax.ShapeDtypeStruct((SUBLANES, t), I32), jax.ShapeDtypeStruct((nb, LANES), I32)],
        compiler_params=_cparams("arbitrary"),
    )(counts, info)


def _row_copy(src, src_row, dst, dst_row, n, sem):
    return pltpu.make_async_copy(src.at[pl.ds(src_row, n)], dst.at[pl.ds(dst_row, n)], sem)


def _dispatch_kernel(dest_ref, hp_ref, xs_in_ref, xs_ref, sem, *, tokens_per_step, n_tokens):
    del xs_in_ref
    base = pl.program_id(0) * tokens_per_step

    def issue(j, carry):
        tok = base + j
        _row_copy(hp_ref, j, xs_ref, dest_ref[tok], 1, sem).start(priority=0)
        _row_copy(hp_ref, j, xs_ref, dest_ref[n_tokens + tok], 1, sem).start(priority=1)
        return carry

    lax.fori_loop(0, tokens_per_step, issue, 0)
    _row_copy(hp_ref, 0, xs_ref, 0, tokens_per_step, sem).wait()
    _row_copy(hp_ref, 0, xs_ref, 0, tokens_per_step, sem).wait()


def _dispatch(dest, hp, n_rows):
    t, wd = hp.shape
    tc = min(t, 512)
    xs0 = jnp.zeros((n_rows, wd), U32)
    return pl.pallas_call(
        functools.partial(_dispatch_kernel, tokens_per_step=tc, n_tokens=t),
        grid_spec=pltpu.PrefetchScalarGridSpec(
            num_scalar_prefetch=1,
            grid=(t // tc,),
            in_specs=[pl.BlockSpec((tc, wd), lambda i, dr: (i, 0)), pl.BlockSpec(memory_space=pl.ANY)],
            out_specs=pl.BlockSpec(memory_space=pl.ANY),
            scratch_shapes=[pltpu.SemaphoreType.DMA(())],
        ),
        out_shape=jax.ShapeDtypeStruct((n_rows, wd), U32),
        input_output_aliases={2: 0},
        compiler_params=_cparams("arbitrary"),
    )(dest, hp, xs0)


def _expert_changed(be_ref):
    i = pl.program_id(0)
    return (i == 0) | (be_ref[i] != be_ref[jnp.maximum(i - 1, 0)])


def _expert_hidden_kernel(be_ref, nu_ref, xs_ref, wg_ref, wu_ref, o_ref, wg_bf, wu_bf):
    @pl.when(pl.program_id(0) < nu_ref[0])
    def _():
        @pl.when(_expert_changed(be_ref))
        def _():
            wg_bf[...] = wg_ref[0].astype(BF16)
            wu_bf[...] = wu_ref[0].astype(BF16)

        u = xs_ref[...]
        half = u.shape[1]
        lo = pltpu.bitcast(u << 16, F32).astype(BF16)
        hi = pltpu.bitcast(u & jnp.uint32(0xFFFF0000), F32).astype(BF16)

        def proj(w_bf):
            return (jnp.dot(lo, w_bf[:half, :], preferred_element_type=F32)
                    + jnp.dot(hi, w_bf[half:, :], preferred_element_type=F32))

        gate = proj(wg_bf)
        o_ref[...] = (gate * _sigmoid(gate) * proj(wu_bf)).astype(o_ref.dtype)

    @pl.when(pl.program_id(0) >= nu_ref[0])
    def _():
        o_ref[...] = jnp.zeros_like(o_ref)


def _expert_down_kernel(be_ref, nu_ref, h_ref, wd_ref, o_ref, wd_bf):
    @pl.when(pl.program_id(0) < nu_ref[0])
    def _():
        @pl.when(_expert_changed(be_ref))
        def _():
            wd_bf[...] = wd_ref[0].astype(BF16)

        o_ref[...] = jnp.dot(h_ref[...], wd_bf[...], preferred_element_type=F32)

    @pl.when(pl.program_id(0) >= nu_ref[0])
    def _():
        o_ref[...] = jnp.zeros_like(o_ref)


def _experts(block_e, n_used, xs, w_e_gate, w_e_up, w_e_down, layer):
    n_rows, half = xs.shape
    d = 2 * half
    hid = w_e_gate.shape[2]
    n_blocks = n_rows // MOE_BLOCK
    blk = lambda i, be, nu: jnp.minimum(i, nu[0] - 1)
    expert = lambda i, be, nu: (layer * N_EXPERTS + be[blk(i, be, nu)], 0, 0)
    hidden = pl.pallas_call(
        _expert_hidden_kernel,
        grid_spec=pltpu.PrefetchScalarGridSpec(
            num_scalar_prefetch=2,
            grid=(n_blocks,),
            in_specs=[pl.BlockSpec((MOE_BLOCK, half), lambda i, be, nu: (blk(i, be, nu), 0)),
                      pl.BlockSpec((1, d, hid), expert),
                      pl.BlockSpec((1, d, hid), expert)],
            out_specs=pl.BlockSpec((MOE_BLOCK, hid), lambda i, be, nu: (i, 0)),
            scratch_shapes=[pltpu.VMEM((d, hid), BF16), pltpu.VMEM((d, hid), BF16)],
        ),
        out_shape=jax.ShapeDtypeStruct((n_rows, hid), BF16),
        compiler_params=_cparams("arbitrary"),
    )(block_e, n_used, xs, w_e_gate, w_e_up)
    return pl.pallas_call(
        _expert_down_kernel,
        grid_spec=pltpu.PrefetchScalarGridSpec(
            num_scalar_prefetch=2,
            grid=(n_blocks,),
            in_specs=[pl.BlockSpec((MOE_BLOCK, hid), lambda i, be, nu: (blk(i, be, nu), 0)),
                      pl.BlockSpec((1, hid, d), expert)],
            out_specs=pl.BlockSpec((MOE_BLOCK, d), lambda i, be, nu: (i, 0)),
            scratch_shapes=[pltpu.VMEM((hid, d), BF16)],
        ),
        out_shape=jax.ShapeDtypeStruct((n_rows, d), F32),
        compiler_params=_cparams("arbitrary"),
    )(block_e, n_used, hidden, w_e_down)


def _combine_kernel(dest_ref, x_ref, info_ref, g_ref, y_ref, *rest, tokens_per_step, n_tokens, final):
    if final:
        o_ref, buf, sem = rest
    else:
        x2_ref, o_ref, buf, sem = rest
    step = pl.program_id(0)

    def gather(tile, slot):
        def issue(j, carry):
            tok = tile * tokens_per_step + j
            _row_copy(y_ref, dest_ref[tok], buf.at[slot, 0], j, 1, sem.at[slot]).start(priority=0)
            _row_copy(y_ref, dest_ref[n_tokens + tok], buf.at[slot, 1], j, 1, sem.at[slot]).start(priority=1)
            return carry

        lax.fori_loop(0, tokens_per_step, issue, 0)

    @pl.when(step == 0)
    def _():
        gather(0, 0)

    @pl.when(step + 1 < pl.num_programs(0))
    def _():
        gather(step + 1, (step + 1) % 2)

    slot = step % 2
    _row_copy(y_ref, 0, buf.at[slot, 0], 0, tokens_per_step, sem.at[slot]).wait()
    _row_copy(y_ref, 0, buf.at[slot, 1], 0, tokens_per_step, sem.at[slot]).wait()
    info = info_ref[...]
    x2 = x_ref[...] + (buf[slot, 0] * info[:, 4:5] + buf[slot, 1] * info[:, 5:6])
    if not final:
        x2_ref[...] = x2
    o_ref[...] = _rms(x2, g_ref[...]).astype(o_ref.dtype)


def _combine(dest, x, info, g_next, y_rows, final):
    t, d = x.shape
    tc = min(t, 256)
    tile = pl.BlockSpec((tc, d), lambda i, dr: (i, 0))
    out_specs = [tile] if final else [tile, tile]
    out_shape = ([jax.ShapeDtypeStruct((t, d), F32)] if final
                 else [jax.ShapeDtypeStruct((t, d), F32), jax.ShapeDtypeStruct((t, d), BF16)])
    return pl.pallas_call(
        functools.partial(_combine_kernel, tokens_per_step=tc, n_tokens=t, final=final),
        grid_spec=pltpu.PrefetchScalarGridSpec(
            num_scalar_prefetch=1,
            grid=(t // tc,),
            in_specs=[tile,
                      pl.BlockSpec((tc, LANES), lambda i, dr: (i, 0)),
                      pl.BlockSpec((1, d), lambda i, dr: (0, 0)),
                      pl.BlockSpec(memory_space=pl.ANY)],
            out_specs=out_specs,
            scratch_shapes=[pltpu.VMEM((2, 2, tc, d), F32), pltpu.SemaphoreType.DMA((2,))],
        ),
        out_shape=out_shape,
        compiler_params=_cparams("arbitrary"),
    )(dest, x, info, g_next.reshape(1, d), y_rows)


def _pad_cols(w, width):
    return jnp.pad(w, ((0, 0), (0, width - w.shape[1])))


def _stage_w_in(w_in, d):
    dsa_heads = d // (4 * HEAD_DIM)
    hpg = d // 1024
    dil_heads = len(DIL_PATTERNS) * hpg
    a_w, b_w = dsa_heads * HEAD_DIM, dil_heads * HEAD_DIM
    pool_w = d - a_w - b_w
    sizes = (a_w, HEAD_DIM, HEAD_DIM, IDX_HEADS * IDX_DIM, IDX_DIM, IDX_HEADS, b_w, b_w, b_w, pool_w)
    offs = [0]
    for s in sizes:
        offs.append(offs[-1] + s)
    cut = offs[6]
    shift = -cut % MXU_WIDTH
    w_al = jnp.concatenate([w_in[:, :cut].astype(BF16), jnp.zeros((w_in.shape[0], shift), BF16),
                            w_in[:, cut:].astype(BF16)], axis=1)
    start = [o + (shift if n >= 6 else 0) for n, o in enumerate(offs)]
    qa, ka, va, qi, ki, wi, qb, kb, vb, c, end = start
    assert all(o % MXU_WIDTH == 0 for o in (qa, ka, qi, ki, qb, kb, vb, c, end))
    assert va == ka + HEAD_DIM and qi == ka + MXU_WIDTH and qb == ki + MXU_WIDTH and wi == ki + IDX_DIM
    tiles = lambda lo, hi: list(range(lo // MXU_WIDTH, hi // MXU_WIDTH))
    t_kav, t_kiw = tiles(ka, qi), tiles(ki, qb)
    calls = dict(rope=tiles(qa, ka) + t_kav + tiles(qb, vb), idx=tiles(qi, ki) + t_kiw,
                 val=t_kav + tiles(vb, c), pool=tiles(c, end) + t_kiw)
    lay = dict(dsa_heads=dsa_heads, dil_heads_per_group=hpg,
               ka=a_w, qb=a_w + MXU_WIDTH, kb=a_w + MXU_WIDTH + b_w,
               ki=IDX_HEADS * IDX_DIM,
               va=HEAD_DIM, vb=MXU_WIDTH,
               wi=pool_w, wi_lane=IDX_DIM)
    return w_al, calls, lay


def kernel(x, positions, norm1_g, w_in, w_gate, b_gate, w_br_a, w_br_b, w_br_c, w_pool, pool_scale,
           w_out, norm2_g, w_rg, b_rg, w_re, b_re, w_e_gate, w_e_up, w_e_down, final_g):
    bsz, seq, d = x.shape
    depth = w_in.shape[0]
    t = bsz * seq
    n_blocks = -(-(2 * t) // MOE_BLOCK) + N_EXPERTS
    n_rows = n_blocks * MOE_BLOCK

    c128, s128, c64, s64 = _rope_tables(positions)
    xt = x.reshape(t, d)
    h = _rms_norm(xt, norm1_g[0], BF16)
    out = None
    for l in range(depth):
        w_al, tiles, lay = _stage_w_in(w_in[l], d)
        zr = _project(h, w_al, tiles["rope"], BF16, rope=(HEAD_DIM, c128, s128))
        zi = _project(h, w_al, tiles["idx"], BF16, rope=(IDX_DIM, c64, s64))
        zv = _project(h, w_al, tiles["val"], BF16)
        zc = _project(h, w_al, tiles["pool"], F32)

        a_out = _dsa(zr, zi, zv, zc, bsz, seq, lay)
        b_out = _dilated(zr, zv, bsz, seq, lay)
        c_out = _pool(zc, w_pool[l].astype(BF16), pool_scale[l], bsz, seq)

        merged = _merge(h, w_gate[l].astype(BF16), b_gate[l], a_out, b_out, c_out,
                        w_br_a[l].astype(BF16), w_br_b[l].astype(BF16), w_br_c[l].astype(BF16))
        x1 = _outproj(merged, w_out[l].astype(BF16), xt)

        w_router = _pad_cols(jnp.concatenate([w_re[l], w_rg[l]], axis=1), LANES).astype(BF16)
        b_router = _pad_cols(jnp.concatenate([b_re[l], b_rg[l]])[None, :], LANES)
        hp, info, counts = _router(x1, norm2_g[l], w_router, b_router)
        dest2, meta = _plan(counts, info, n_blocks)
        dest = dest2[:2].reshape(2 * t)
        block_e = meta[:n_blocks, 0]
        n_used = meta[:1, 1]
        xs = _dispatch(dest, hp, n_rows)
        stack = lambda w: w.reshape((depth * N_EXPERTS,) + w.shape[2:])
        y_rows = _experts(block_e, n_used, xs, stack(w_e_gate), stack(w_e_up), stack(w_e_down), l)
        final = l == depth - 1
        g_next = final_g if final else norm1_g[l + 1]
        res = _combine(dest, x1, info, g_next, y_rows, final)
        if final:
            out = res[0]
        else:
            xt, h = res
    return out.reshape(bsz, seq, d)
```
